```python
import math
import jax, jax.numpy as jnp
from jax import lax
import numpy as np

D_MODEL = 4096
BATCH = 4
SEQ = 2048
DEPTH = 4
DEC_BATCH = 8
DEC_SEQ = 1
PAST_LEN = 8192
PAGE_SIZE = 128

N_EVEN = (DEPTH + 1) // 2
N_ODD = DEPTH // 2
HEAD_DIM = 128
A_HEADS = D_MODEL // (2 * HEAD_DIM)
A_WIDTH = A_HEADS * HEAD_DIM
B_WIDTH = D_MODEL - A_WIDTH
B_GROUPS = 16
B_GDIM = B_WIDTH // B_GROUPS
CHUNK = 128
Q_BLOCK = 128
SB_BIAS_INIT = -7.0
C_EXPAND = 128
C_HEADS = D_MODEL // C_EXPAND
C_DK = C_EXPAND
C_DV = D_MODEL // C_HEADS
C_WIDTH = C_HEADS * C_DK
C_VWIDTH = C_HEADS * C_DV
GLA_CHUNK = 64
MEM_LEN = 256
X_HEADS = 4
X_HEAD_DIM = 128
X_WIDTH = X_HEADS * X_HEAD_DIM
FFN_HIDDEN = -(-8 * D_MODEL // (3 * 256)) * 256
EPS = 1e-6

kernel_name = "stickbreak_gmlp_hgrn2_hybrid_step"


def rmsnorm(x, g):
    xf = x.astype(jnp.float32)
    r = lax.rsqrt(jnp.mean(xf * xf, axis=-1, keepdims=True) + EPS)
    return (xf * r).astype(x.dtype) * g


def stick_breaking_block(q, q_pos, k, v, k_pos, bias):
    z = jnp.einsum('bqhd,bkhd->bhqk', q, k).astype(jnp.float32) * (HEAD_DIM ** -0.5)
    z = z + bias.astype(jnp.float32)[None, :, None, None]
    mask = (k_pos[None, :] < q_pos[:, None])[None, None]
    log_keep = jnp.where(mask, jax.nn.log_sigmoid(-z), 0.0)
    after = lax.cumsum(log_keep, axis=3, reverse=True) - log_keep
    w = jnp.where(mask, jnp.exp(jax.nn.log_sigmoid(z) + after), 0.0)
    return jnp.einsum('bhqk,bkhd->bqhd', w.astype(v.dtype), v)


def stick_breaking_prompt(q, k, v, bias):
    B, T, H, d = q.shape
    k_pos = jnp.arange(T)

    def one(i):
        start = i * Q_BLOCK
        qb = lax.dynamic_slice_in_dim(q, start, Q_BLOCK, axis=1)
        return stick_breaking_block(qb, start + jnp.arange(Q_BLOCK), k, v, k_pos, bias)

    o = lax.map(one, jnp.arange(T // Q_BLOCK))
    return jnp.moveaxis(o, 0, 1).reshape(B, T, H * d)


def causal_spatial(ws):
    return ws * jnp.tril(jnp.ones((CHUNK, CHUNK), ws.dtype))


def gmlp_prompt(u, vn, ws, bs):
    B, T, _ = u.shape
    vc = vn.reshape(B, T // CHUNK, CHUNK, B_GROUPS, B_GDIM)
    s = jnp.einsum('gts,bcsgd->bctgd', causal_spatial(ws), vc) + bs.T[None, None, :, :, None]
    return u * s.reshape(B, T, B_WIDTH)


def gmlp_sample(u, vn, ws, bs):
    B, T, _ = u.shape
    w = causal_spatial(ws)[:, :T, :T]
    vc = vn.reshape(B, T, B_GROUPS, B_GDIM)
    s = jnp.einsum('gts,bsgd->btgd', w, vc) + bs[:, :T].T[None, :, :, None]
    return u * s.reshape(B, T, B_WIDTH)


def ab_project(h, w_in, gnorm):
    B, T, _ = h.shape
    p = h @ w_in
    q, k, v, u, vb = jnp.split(p, [A_WIDTH, 2 * A_WIDTH, 3 * A_WIDTH, 3 * A_WIDTH + B_WIDTH], axis=-1)
    hd = lambda t: t.reshape(B, T, A_HEADS, HEAD_DIM)
    u = jax.nn.gelu(u)
    vn = rmsnorm(jax.nn.gelu(vb), gnorm)
    return hd(q), hd(k), hd(v), u, vn


def hgrn_project(h, w_in, lb):
    B, T, _ = h.shape
    p = h @ w_in
    q, fpre, i, g = jnp.split(p, [C_WIDTH, 2 * C_WIDTH, 2 * C_WIDTH + C_VWIDTH], axis=-1)
    lbh = lb.reshape(C_HEADS, C_DK)
    fpre = fpre.astype(jnp.float32).reshape(B, T, C_HEADS, C_DK)
    logf = jnp.logaddexp(jnp.log(lbh), jnp.log1p(-lbh) + jax.nn.log_sigmoid(fpre))
    k = -jnp.expm1(logf)
    q = q.astype(jnp.float32).reshape(B, T, C_HEADS, C_DK)
    v = i.astype(jnp.float32).reshape(B, T, C_HEADS, C_DV)
    return q, k, v, logf, g


def gla_chunk(S, q, k, v, logf):
    C = q.shape[1]
    b = jnp.cumsum(logf, axis=1)
    causal = jnp.tril(jnp.ones((C, C), bool))[None, :, :, None, None]
    decay = jnp.exp(jnp.where(causal, b[:, :, None] - b[:, None, :], -jnp.inf))
    scores = jnp.einsum('bthd,bshd,btshd->bhts', q, k, decay)
    o = jnp.einsum('bhts,bshv->bthv', scores, v) + jnp.einsum('bthd,bhdv->bthv', q * jnp.exp(b), S)
    b_last = b[:, -1]
    S_new = jnp.exp(b_last)[..., None] * S + jnp.einsum('bshd,bshv->bhdv', k * jnp.exp(b_last[:, None] - b), v)
    return S_new, o


def hgrn_prompt(q, k, v, logf):
    B, T, H, _ = q.shape
    nc = T // GLA_CHUNK
    split = lambda t: jnp.moveaxis(t.reshape(B, nc, GLA_CHUNK, H, t.shape[-1]), 1, 0)
    S0 = jnp.zeros((B, H, C_DK, C_DV), jnp.float32)
    S, o = lax.scan(lambda S, xs: gla_chunk(S, *xs), S0, (split(q), split(k), split(v), split(logf)))
    return S, jnp.moveaxis(o, 0, 1).reshape(B, T, H, C_DV)


def hgrn_out(o, g, gnorm, w_out, dtype):
    B, T = o.shape[:2]
    o = rmsnorm(o, gnorm) * jax.nn.silu(g.astype(jnp.float32).reshape(B, T, C_HEADS, C_DV))
    return o.reshape(B, T, C_VWIDTH).astype(dtype) @ w_out


def mem_kv(mem, g, w_kv):
    B, M, _ = mem.shape
    k, v = jnp.split(rmsnorm(mem, g) @ w_kv, 2, axis=-1)
    return k.reshape(B, M, X_HEADS, X_HEAD_DIM), v.reshape(B, M, X_HEADS, X_HEAD_DIM)


def cross_attend(h, k, v, w_q, w_o):
    B, T, _ = h.shape
    q = (h @ w_q).reshape(B, T, X_HEADS, X_HEAD_DIM)
    s = jnp.einsum('bthd,bmhd->bhtm', q, k).astype(jnp.float32) * (X_HEAD_DIM ** -0.5)
    p = jax.nn.softmax(s, axis=-1).astype(v.dtype)
    return jnp.einsum('bhtm,bmhd->bthd', p, v).reshape(B, T, X_WIDTH) @ w_o


def swiglu(h, w13, w2):
    a, b = jnp.split(h @ w13, 2, axis=-1)
    return (jax.nn.silu(a) * b) @ w2


def setup_inputs(seed: int = 0) -> dict:
    key = jax.random.key(seed)
    ks = iter(jax.random.split(key, 40))
    nrm = lambda shape, scale: scale * jax.random.normal(next(ks), shape, jnp.float32)
    gain = lambda shape: 1.0 + nrm(shape, 0.05)
    n_pages = PAST_LEN // PAGE_SIZE
    n_used = DEC_BATCH * n_pages
    n_phys = n_used + max(1, n_used // 4)
    page_table = jax.random.permutation(next(ks), n_phys)[:n_used].reshape(DEC_BATCH, n_pages).astype(jnp.int32)
    return dict(
        x_prompt=nrm((BATCH, SEQ, D_MODEL), 1.0),
        x_sample=nrm((DEC_BATCH, DEC_SEQ, D_MODEL), 1.0),
        mem_prompt=nrm((BATCH, MEM_LEN, D_MODEL), 1.0),
        cache_sb_k=nrm((n_phys, N_EVEN, PAGE_SIZE, A_HEADS, HEAD_DIM), 1.0),
        cache_sb_v=nrm((n_phys, N_EVEN, PAGE_SIZE, A_HEADS, HEAD_DIM), 1.0),
        cache_mem_k=nrm((DEPTH, DEC_BATCH, MEM_LEN, X_HEADS, X_HEAD_DIM), 1.0),
        cache_mem_v=nrm((DEPTH, DEC_BATCH, MEM_LEN, X_HEADS, X_HEAD_DIM), 1.0),
        state_hgrn=nrm((N_ODD, DEC_BATCH, C_HEADS, C_DK, C_DV), 0.3),
        page_table=page_table,
        norm_mix=gain((DEPTH, D_MODEL)),
        norm_mem=gain((DEPTH, D_MODEL)),
        norm_xattn=gain((DEPTH, D_MODEL)),
        norm_ffn=gain((DEPTH, D_MODEL)),
        norm_final=gain((D_MODEL,)),
        w_in_ab=nrm((N_EVEN, D_MODEL, 3 * A_WIDTH + 2 * B_WIDTH), D_MODEL ** -0.5),
        w_out_ab=nrm((N_EVEN, A_WIDTH + B_WIDTH, D_MODEL), (A_WIDTH + B_WIDTH) ** -0.5),
        sb_bias=SB_BIAS_INIT + nrm((N_EVEN, A_HEADS), 0.3),
        gmlp_norm=gain((N_EVEN, B_WIDTH)),
        gmlp_ws=nrm((N_EVEN, B_GROUPS, CHUNK, CHUNK), CHUNK ** -0.5),
        gmlp_bs=1.0 + nrm((N_EVEN, B_GROUPS, CHUNK), 0.1),
        w_in_c=nrm((N_ODD, D_MODEL, 2 * C_WIDTH + 2 * C_VWIDTH), D_MODEL ** -0.5),
        w_out_c=nrm((N_ODD, C_VWIDTH, D_MODEL), C_VWIDTH ** -0.5),
        hgrn_lb=nrm((N_ODD, C_WIDTH), 0.5),
        hgrn_gnorm=gain((N_ODD, C_DV)),
        xattn_wq=nrm((DEPTH, D_MODEL, X_WIDTH), D_MODEL ** -0.5),
        xattn_wkv=nrm((DEPTH, D_MODEL, 2 * X_WIDTH), D_MODEL ** -0.5),
        xattn_wo=nrm((DEPTH, X_WIDTH, D_MODEL), X_WIDTH ** -0.5),
        ffn_w13=nrm((DEPTH, D_MODEL, 2 * FFN_HIDDEN), D_MODEL ** -0.5),
        ffn_w2=nrm((DEPTH, FFN_HIDDEN, D_MODEL), FFN_HIDDEN ** -0.5),
    )


def reference(x_prompt, x_sample, mem_prompt, cache_sb_k, cache_sb_v, cache_mem_k, cache_mem_v,
              state_hgrn, page_table, norm_mix, norm_mem, norm_xattn, norm_ffn, norm_final,
              w_in_ab, w_out_ab, sb_bias, gmlp_norm, gmlp_ws, gmlp_bs, w_in_c, w_out_c, hgrn_lb, hgrn_gnorm,
              xattn_wq, xattn_wkv, xattn_wo, ffn_w13, ffn_w2):
    xp, xs = x_prompt, x_sample
    Bp, Tp = xp.shape[:2]
    Bd, Ts = xs.shape[:2]
    pos_new = PAST_LEN + jnp.arange(Ts)
    k_pos_all = jnp.arange(PAST_LEN + Ts)
    lb_all = jnp.cumsum(jax.nn.softmax(hgrn_lb.astype(jnp.float32), axis=0), axis=0)
    lb_all = lb_all - lb_all[:1]
    sbk_p, sbv_p, sbk_s, sbv_s, gv_s, hs_p, hs_s, mk_p, mv_p = [], [], [], [], [], [], [], [], []
    for l in range(DEPTH):
        j = l // 2
        hp = rmsnorm(xp, norm_mix[l])
        hs = rmsnorm(xs, norm_mix[l])
        if l % 2 == 0:
            q, k, v, u, vn = ab_project(hp, w_in_ab[j], gmlp_norm[j])
            oa = stick_breaking_prompt(q, k, v, sb_bias[j])
            ob = gmlp_prompt(u, vn, gmlp_ws[j], gmlp_bs[j])
            xp = xp + jnp.concatenate([oa, ob], axis=-1) @ w_out_ab[j]
            sbk_p.append(k)
            sbv_p.append(v)
            q, k, v, u, vn = ab_project(hs, w_in_ab[j], gmlp_norm[j])
            k_past = cache_sb_k[page_table, j].reshape(Bd, PAST_LEN, A_HEADS, HEAD_DIM).astype(k.dtype)
            v_past = cache_sb_v[page_table, j].reshape(Bd, PAST_LEN, A_HEADS, HEAD_DIM).astype(v.dtype)
            k_all = jnp.concatenate([k_past, k], axis=1)
            v_all = jnp.concatenate([v_past, v], axis=1)
            oa = stick_breaking_block(q, pos_new, k_all, v_all, k_pos_all, sb_bias[j]).reshape(Bd, Ts, A_WIDTH)
            ob = gmlp_sample(u, vn, gmlp_ws[j], gmlp_bs[j])
            xs = xs + jnp.concatenate([oa, ob], axis=-1) @ w_out_ab[j]
            sbk_s.append(k)
            sbv_s.append(v)
            gv_s.append(vn.reshape(Bd, Ts, B_GROUPS, B_GDIM))
        else:
            q, k, v, logf, g = hgrn_project(hp, w_in_c[j], lb_all[j])
            S_p, o = hgrn_prompt(q, k, v, logf)
            xp = xp + hgrn_out(o, g, hgrn_gnorm[j], w_out_c[j], xp.dtype)
            hs_p.append(S_p)
            q, k, v, logf, g = hgrn_project(hs, w_in_c[j], lb_all[j])
            S_s, o = gla_chunk(state_hgrn[j].astype(jnp.float32), q, k, v, logf)
            xs = xs + hgrn_out(o, g, hgrn_gnorm[j], w_out_c[j], xs.dtype)
            hs_s.append(S_s)
        kp, vp = mem_kv(mem_prompt, norm_mem[l], xattn_wkv[l])
        xp = xp + cross_attend(rmsnorm(xp, norm_xattn[l]), kp, vp, xattn_wq[l], xattn_wo[l])
        xs = xs + cross_attend(rmsnorm(xs, norm_xattn[l]), cache_mem_k[l].astype(xs.dtype),
                               cache_mem_v[l].astype(xs.dtype), xattn_wq[l], xattn_wo[l])
        mk_p.append(kp)
        mv_p.append(vp)
        xp = xp + swiglu(rmsnorm(xp, norm_ffn[l]), ffn_w13[l], ffn_w2[l])
        xs = xs + swiglu(rmsnorm(xs, norm_ffn[l]), ffn_w13[l], ffn_w2[l])
    y_prompt = rmsnorm(xp, norm_final)
    y_sample = rmsnorm(xs, norm_final)
    sb_k_prompt = jnp.stack(sbk_p, axis=1)
    sb_v_prompt = jnp.stack(sbv_p, axis=1)
    sb_k_sample = jnp.stack(sbk_s, axis=1)
    sb_v_sample = jnp.stack(sbv_s, axis=1)
    gmlp_v_sample = jnp.stack(gv_s, axis=1)
    hgrn_state_prompt = jnp.stack(hs_p, axis=0)
    hgrn_state_sample = jnp.stack(hs_s, axis=0)
    mem_k_prompt = jnp.stack(mk_p, axis=0)
    mem_v_prompt = jnp.stack(mv_p, axis=0)
    return (y_prompt, y_sample, sb_k_prompt, sb_v_prompt, sb_k_sample, sb_v_sample, gmlp_v_sample,
            hgrn_state_prompt, hgrn_state_sample, mem_k_prompt, mem_v_prompt)
```

```python
import functools

import jax
import jax.numpy as jnp
from jax import lax
from jax.experimental import pallas as pl
from jax.experimental.pallas import tpu as pltpu

F32 = jnp.float32
BF16 = jnp.bfloat16
EPS = 1e-6
LANES = 128
SUB = 16
VMEM_CAP = 60 * 1024 * 1024
VMEM_SLACK = 12 * 1024 * 1024


def _nbytes(shape, dtype):
    n = 1
    for s in shape:
        n *= s
    return n * jnp.dtype(dtype).itemsize


def _params(semantics, blocks):
    need = 2 * sum(_nbytes(s, d) for s, d in blocks) + VMEM_SLACK
    return pltpu.CompilerParams(dimension_semantics=semantics,
                                vmem_limit_bytes=int(min(need, VMEM_CAP)))


def _rms_rows(x, g):
    r = lax.rsqrt(jnp.mean(x * x, axis=-1, keepdims=True) + EPS)
    return (x * r) * g


def _split_dot(lhs, rhs, split, terms):
    x = lhs if split == "lhs" else rhs
    acc = None
    for _ in range(terms):
        piece = x.astype(BF16)
        a, b = (piece, rhs) if split == "lhs" else (lhs, piece)
        part = jnp.dot(a, b, preferred_element_type=F32)
        acc = part if acc is None else acc + part
        x = x - piece.astype(F32)
    return acc


def _neg_softplus(z):
    return -(jnp.maximum(z, 0.0) + jnp.log1p(jnp.exp(-jnp.abs(z))))


def _norm_body(x_ref, g_ref, o_ref):
    o_ref[...] = _rms_rows(x_ref[...], g_ref[...]).astype(o_ref.dtype)


def _rmsnorm(x, g, out_dtype, name):
    m, d = x.shape
    tm = min(m, 256)
    return pl.pallas_call(
        _norm_body,
        grid=(m // tm,),
        in_specs=[pl.BlockSpec((tm, d), lambda i: (i, 0)),
                  pl.BlockSpec((1, d), lambda i: (0, 0))],
        out_specs=pl.BlockSpec((tm, d), lambda i: (i, 0)),
        out_shape=jax.ShapeDtypeStruct((m, d), out_dtype),
        compiler_params=_params(("parallel",), [((tm, d), F32), ((tm, d), out_dtype)]),
        name=name,
    )(x, g.reshape(1, d))


def _mm_body(*refs, k_sizes, has_resid):
    n_a = len(k_sizes)
    a_refs, w_ref, o_ref = refs[:n_a], refs[n_a], refs[-1]
    acc, off = None, 0
    for a_ref, kk in zip(a_refs, k_sizes):
        part = jnp.dot(a_ref[...], w_ref[off:off + kk, :], preferred_element_type=F32)
        acc = part if acc is None else acc + part
        off += kk
    if has_resid:
        acc = acc + refs[n_a + 1][...]
    o_ref[...] = acc.astype(o_ref.dtype)


def _matmul(a_list, w, *, out_dtype, tm, tn, resid=None, name):
    m = a_list[0].shape[0]
    k_sizes = tuple(a.shape[1] for a in a_list)
    k, n = w.shape
    assert sum(k_sizes) == k and m % tm == 0 and n % tn == 0
    in_specs = [pl.BlockSpec((tm, kk), lambda j, i: (i, 0)) for kk in k_sizes]
    in_specs.append(pl.BlockSpec((k, tn), lambda j, i: (0, j)))
    blocks = [((tm, kk), BF16) for kk in k_sizes] + [((k, tn), BF16), ((tm, tn), out_dtype)]
    args = list(a_list) + [w]
    if resid is not None:
        in_specs.append(pl.BlockSpec((tm, tn), lambda j, i: (i, j)))
        blocks.append(((tm, tn), F32))
        args.append(resid)
    return pl.pallas_call(
        functools.partial(_mm_body, k_sizes=k_sizes, has_resid=resid is not None),
        grid=(n // tn, m // tm),
        in_specs=in_specs,
        out_specs=pl.BlockSpec((tm, tn), lambda j, i: (i, j)),
        out_shape=jax.ShapeDtypeStruct((m, n), out_dtype),
        compiler_params=_params(("parallel", "parallel"), blocks),
        name=name,
    )(*args)


def _swiglu_body(a_ref, wa_ref, wb_ref, o_ref):
    a = a_ref[...]
    ga = jnp.dot(a, wa_ref[...], preferred_element_type=F32)
    gb = jnp.dot(a, wb_ref[...], preferred_element_type=F32)
    o_ref[...] = (jax.nn.silu(ga) * gb).astype(o_ref.dtype)


def _matmul_swiglu(a, w13, *, tm, tn, name):
    m, k = a.shape
    hidden = w13.shape[1] // 2
    assert m % tm == 0 and hidden % tn == 0
    nj = hidden // tn
    return pl.pallas_call(
        _swiglu_body,
        grid=(nj, m // tm),
        in_specs=[pl.BlockSpec((tm, k), lambda j, i: (i, 0)),
                  pl.BlockSpec((k, tn), lambda j, i: (0, j)),
                  pl.BlockSpec((k, tn), lambda j, i: (0, j + nj))],
        out_specs=pl.BlockSpec((tm, tn), lambda j, i: (i, j)),
        out_shape=jax.ShapeDtypeStruct((m, hidden), BF16),
        compiler_params=_params(("parallel", "parallel"),
                                [((tm, k), BF16), ((k, tn), BF16), ((k, tn), BF16), ((tm, tn), BF16)]),
        name=name,
    )(a, w13, w13)


def _sb_prompt_body(bias_ref, q_ref, k_ref, v_ref, o_ref, *, tq, scale):
    qi = pl.program_id(2)
    q = q_ref[...].astype(BF16)
    bias = bias_ref[...]
    row = lax.broadcasted_iota(jnp.int32, (tq, tq), 0)
    col = lax.broadcasted_iota(jnp.int32, (tq, tq), 1)
    later = (row > col).astype(BF16)
    causal = col < row

    def block(kb, acc, carry, mask):
        ks = k_ref[pl.ds(kb * tq, tq), :].astype(BF16)
        vs = v_ref[pl.ds(kb * tq, tq), :].astype(BF16)
        z = lax.dot_general(q, ks, (((1,), (1,)), ((), ())), preferred_element_type=F32) * scale + bias
        log_keep = _neg_softplus(z)
        if mask:
            log_keep = jnp.where(causal, log_keep, 0.0)
        after = _split_dot(log_keep, later, "lhs", 2)
        w = jnp.exp(log_keep + z + after + carry)
        if mask:
            w = jnp.where(causal, w, 0.0)
        acc = acc + jnp.dot(w.astype(BF16), vs, preferred_element_type=F32)
        carry = carry + after[:, :1] + log_keep[:, :1]
        return acc, carry

    acc0 = jnp.zeros((tq, q.shape[1]), F32)
    carry0 = jnp.zeros((tq, 1), F32)
    acc, carry = block(qi, acc0, carry0, True)

    def body(it, c):
        return block(qi - 1 - it, c[0], c[1], False)

    acc, _ = lax.fori_loop(0, qi, body, (acc, carry))
    o_ref[...] = acc.astype(o_ref.dtype)


def _sb_prompt(p, bias, *, batch, seq, heads, tq, name):
    d = LANES
    nq = seq // tq
    bias_rows = jnp.broadcast_to(bias.astype(F32)[:, None, None], (heads, 1, tq))
    return pl.pallas_call(
        functools.partial(_sb_prompt_body, tq=tq, scale=d ** -0.5),
        grid=(batch, heads, nq),
        in_specs=[pl.BlockSpec((None, 1, tq), lambda b, h, i: (h, 0, 0)),
                  pl.BlockSpec((tq, d), lambda b, h, i: (b * nq + i, h)),
                  pl.BlockSpec((seq, d), lambda b, h, i: (b, heads + h)),
                  pl.BlockSpec((seq, d), lambda b, h, i: (b, 2 * heads + h))],
        out_specs=pl.BlockSpec((tq, d), lambda b, h, i: (b * nq + i, h)),
        out_shape=jax.ShapeDtypeStruct((batch * seq, heads * d), BF16),
        compiler_params=_params(("parallel", "parallel", "parallel"),
                                [((tq, d), F32), ((seq, d), F32), ((seq, d), F32), ((tq, d), BF16)]),
        name=name,
    )(bias_rows, p, p, p)


def _sb_sample_body(pt_ref, q_ref, bias_ref, k_ref, v_ref, o_ref, acc_ref, carry_ref, *, heads, scale):
    del pt_ref
    pg = pl.program_id(1)
    page = k_ref.shape[0]

    @pl.when(pg == 0)
    def _():
        acc_ref[...] = jnp.zeros_like(acc_ref)
        carry_ref[...] = jnp.zeros_like(carry_ref)

    row = lax.broadcasted_iota(jnp.int32, (page, page), 0)
    col = lax.broadcasted_iota(jnp.int32, (page, page), 1)
    later = (row > col).astype(BF16)
    z = lax.dot_general(q_ref[...], k_ref[...].astype(BF16), (((1,), (1,)), ((), ())),
                        preferred_element_type=F32) * scale + bias_ref[...]
    log_keep = _neg_softplus(z)
    after = _split_dot(log_keep, later, "lhs", 3)
    w = jnp.exp(log_keep + z + after + carry_ref[...])
    acc_ref[...] += jnp.dot(w.astype(BF16), v_ref[...].astype(BF16), preferred_element_type=F32)
    carry_ref[...] += after[:, :1] + log_keep[:, :1]

    @pl.when(pg == pl.num_programs(1) - 1)
    def _():
        acc = acc_ref[...]
        hrow = lax.broadcasted_iota(jnp.int32, acc.shape, 0)
        hcol = lax.broadcasted_iota(jnp.int32, acc.shape, 1) // LANES
        o_ref[...] = jnp.sum(jnp.where(hrow == hcol, acc, 0.0), axis=0, keepdims=True).astype(o_ref.dtype)


def _sb_sample(q, bias, cache_k, cache_v, page_table, layer, name):
    bsz, heads, d = q.shape
    n_phys, n_layers, page = cache_k.shape[:3]
    n_pages = page_table.shape[1]
    width = heads * d
    ck = cache_k.reshape(n_phys, n_layers, page, width)
    cv = cache_v.reshape(n_phys, n_layers, page, width)
    eye = jnp.eye(heads, dtype=F32)
    q_bd = (q[:, :, None, :] * eye[None, :, :, None]).reshape(bsz, heads, width).astype(BF16)
    bias_rows = jnp.broadcast_to(bias.astype(F32)[:, None], (heads, page))
    cache_map = lambda b, p, pt: (pt[b, n_pages - 1 - p], layer, 0, 0)
    grid_spec = pltpu.PrefetchScalarGridSpec(
        num_scalar_prefetch=1,
        grid=(bsz, n_pages),
        in_specs=[pl.BlockSpec((None, heads, width), lambda b, p, pt: (b, 0, 0)),
                  pl.BlockSpec((heads, page), lambda b, p, pt: (0, 0)),
                  pl.BlockSpec((None, None, page, width), cache_map),
                  pl.BlockSpec((None, None, page, width), cache_map)],
        out_specs=pl.BlockSpec((None, 1, width), lambda b, p, pt: (b, 0, 0)),
        scratch_shapes=[pltpu.VMEM((heads, width), F32), pltpu.VMEM((heads, 1), F32)],
    )
    out = pl.pallas_call(
        functools.partial(_sb_sample_body, heads=heads, scale=d ** -0.5),
        grid_spec=grid_spec,
        out_shape=jax.ShapeDtypeStruct((bsz, 1, width), BF16),
        compiler_params=_params(("parallel", "arbitrary"),
                                [((page, width), F32), ((page, width), F32), ((heads, width), BF16)]),
        name=name,
    )(page_table, q_bd, bias_rows, ck, cv)
    return out.reshape(bsz, width)


def _gmlp_prompt_body(u_ref, vb_ref, gn_ref, ws_ref, bst_ref, o_ref, *, groups):
    chunk = u_ref.shape[0]
    vn = _rms_rows(jax.nn.gelu(vb_ref[...]), gn_ref[...])
    row = lax.broadcasted_iota(jnp.int32, (chunk, chunk), 0)
    col = lax.broadcasted_iota(jnp.int32, (chunk, chunk), 1)
    for g in range(groups):
        sl = slice(g * LANES, (g + 1) * LANES)
        w = jnp.where(col <= row, ws_ref[g], 0.0).astype(BF16)
        s = jnp.dot(w, vn[:, sl].astype(BF16), preferred_element_type=F32) + bst_ref[:, g:g + 1]
        o_ref[:, sl] = (jax.nn.gelu(u_ref[:, sl]) * s).astype(o_ref.dtype)


def _gmlp_prompt(p, gnorm, ws, bs, *, u_col, name):
    groups, chunk, _ = ws.shape
    width = groups * LANES
    m = p.shape[0]
    return pl.pallas_call(
        functools.partial(_gmlp_prompt_body, groups=groups),
        grid=(m // chunk,),
        in_specs=[pl.BlockSpec((chunk, width), lambda i: (i, u_col)),
                  pl.BlockSpec((chunk, width), lambda i: (i, u_col + 1)),
                  pl.BlockSpec((1, width), lambda i: (0, 0)),
                  pl.BlockSpec((groups, chunk, chunk), lambda i: (0, 0, 0)),
                  pl.BlockSpec((chunk, groups), lambda i: (0, 0))],
        out_specs=pl.BlockSpec((chunk, width), lambda i: (i, 0)),
        out_shape=jax.ShapeDtypeStruct((m, width), BF16),
        compiler_params=_params(("parallel",),
                                [((chunk, width), F32)] * 2 + [((groups, chunk, chunk), F32), ((chunk, width), BF16)]),
        name=name,
    )(p, p, gnorm.reshape(1, width), ws, bs.T)


def _gmlp_sample_body(u_ref, vb_ref, gn_ref, w0_ref, b0_ref, o_ref, vn_ref):
    vn = _rms_rows(jax.nn.gelu(vb_ref[...]), gn_ref[...])
    vn_ref[...] = vn
    o_ref[...] = (jax.nn.gelu(u_ref[...]) * (w0_ref[...] * vn + b0_ref[...])).astype(o_ref.dtype)


def _gmlp_sample(p, gnorm, ws, bs, *, u_col, name):
    groups = ws.shape[0]
    width = groups * LANES
    m = p.shape[0]
    w0 = jnp.repeat(ws[:, 0, 0], LANES).reshape(1, width)
    b0 = jnp.repeat(bs[:, 0], LANES).reshape(1, width)
    row = lambda c: pl.BlockSpec((m, width), lambda i: (0, c))
    vec = pl.BlockSpec((1, width), lambda i: (0, 0))
    return pl.pallas_call(
        _gmlp_sample_body,
        grid=(1,),
        in_specs=[row(u_col), row(u_col + 1), vec, vec, vec],
        out_specs=[row(0), row(0)],
        out_shape=[jax.ShapeDtypeStruct((m, width), BF16), jax.ShapeDtypeStruct((m, width), F32)],
        compiler_params=_params(("arbitrary",), [((m, width), F32)] * 4),
        name=name,
    )(p, p, gnorm.reshape(1, width), w0, b0)


def _hgrn_gates(fpre, lb):
    logf = jnp.logaddexp(jnp.log(lb), jnp.log1p(-lb) + jax.nn.log_sigmoid(fpre))
    return logf, (1.0 - lb) * jax.nn.sigmoid(-fpre)


def _hgrn_finish(o, g, gn):
    return _rms_rows(o, gn) * jax.nn.silu(g)


def _hgrn_prompt_body(q_ref, f_ref, i_ref, g_ref, lb_ref, gn_ref, o_ref, s_ref, st_ref, *, tb):
    t = pl.program_id(2)
    nsub = tb // SUB
    d = LANES

    @pl.when(t == 0)
    def _():
        st_ref[...] = jnp.zeros_like(st_ref)

    q = q_ref[...]
    v = i_ref[...]
    logf, k = _hgrn_gates(f_ref[...], lb_ref[...])

    row = lax.broadcasted_iota(jnp.int32, (tb, tb), 0)
    col = lax.broadcasted_iota(jnp.int32, (tb, tb), 1)
    same = (row // SUB) == (col // SUB)
    b = _split_dot((same & (col <= row)).astype(BF16), logf, "rhs", 3)
    btot = _split_dot(same.astype(BF16), logf, "rhs", 3)
    qt = q * jnp.exp(b)
    kt = k * jnp.exp(btot - b)
    dec = jnp.exp(btot)

    shp = (nsub, SUB, d)
    q3, k3, v3, b3 = q.reshape(shp), k.reshape(shp), v.reshape(shp), b.reshape(shp)
    tloc = lax.broadcasted_iota(jnp.int32, shp, 1)
    ones = jnp.ones((d, d), BF16)
    od = jnp.zeros(shp, F32)
    for s in range(SUB):
        delta = jnp.where(tloc >= s, b3 - b3[:, s:s + 1, :], -1e30)
        prod = jnp.exp(delta) * (q3 * k3[:, s:s + 1, :])
        score = jnp.dot(prod.reshape(tb, d).astype(BF16), ones, preferred_element_type=F32)
        od = od + score.reshape(shp) * v3[:, s:s + 1, :]

    st = st_ref[...]
    parts = []
    for n in range(nsub):
        sl = slice(n * SUB, (n + 1) * SUB)
        parts.append(lax.dot_general(qt[sl].astype(BF16), st.astype(BF16), (((1,), (1,)), ((), ())),
                                     preferred_element_type=F32))
        upd = lax.dot_general(v[sl].astype(BF16), kt[sl].astype(BF16), (((0,), (0,)), ((), ())),
                              preferred_element_type=F32)
        st = st * dec[n * SUB:n * SUB + 1, :] + upd
    st_ref[...] = st
    o = od.reshape(tb, d) + jnp.concatenate(parts, axis=0)
    o_ref[...] = _hgrn_finish(o, g_ref[...], gn_ref[...]).astype(o_ref.dtype)

    @pl.when(t == pl.num_programs(2) - 1)
    def _():
        s_ref[...] = st.T


def _hgrn_prompt(p, lb, gnorm, *, batch, seq, heads, tb, name):
    d = LANES
    nt = seq // tb
    col = lambda c: pl.BlockSpec((tb, d), lambda b, h, t: (b * nt + t, c * heads + h))
    return pl.pallas_call(
        functools.partial(_hgrn_prompt_body, tb=tb),
        grid=(batch, heads, nt),
        in_specs=[col(0), col(1), col(2), col(3),
                  pl.BlockSpec((None, 1, d), lambda b, h, t: (h, 0, 0)),
                  pl.BlockSpec((1, d), lambda b, h, t: (0, 0))],
        out_specs=[pl.BlockSpec((tb, d), lambda b, h, t: (b * nt + t, h)),
                   pl.BlockSpec((None, None, d, d), lambda b, h, t: (b, h, 0, 0))],
        out_shape=[jax.ShapeDtypeStruct((batch * seq, heads * d), BF16),
                   jax.ShapeDtypeStruct((batch, heads, d, d), F32)],
        scratch_shapes=[pltpu.VMEM((d, d), F32)],
        compiler_params=_params(("parallel", "parallel", "arbitrary"), [((tb, d), F32)] * 5 + [((d, d), F32)]),
        name=name,
    )(p, p, p, p, lb.reshape(heads, 1, d), gnorm.reshape(1, d))


def _hgrn_sample_body(q_ref, f_ref, i_ref, g_ref, lb_ref, gn_ref, s_ref, o_ref, so_ref, oacc_ref, *, heads):
    d = LANES
    q = q_ref[...]
    v = i_ref[...]
    logf, k = _hgrn_gates(f_ref[...], lb_ref[...])
    f = jnp.exp(logf)
    eye = lax.broadcasted_iota(jnp.int32, (d, d), 0) == lax.broadcasted_iota(jnp.int32, (d, d), 1)

    def to_col(r):
        return jnp.sum(jnp.where(eye, r, 0.0), axis=1, keepdims=True)

    for h in range(heads):
        hs = slice(h, h + 1)
        s_new = to_col(f[hs]) * s_ref[h] + to_col(k[hs]) * v[hs]
        so_ref[h] = s_new
        oacc_ref[hs, :] = jnp.sum(to_col(q[hs]) * s_new, axis=0, keepdims=True)
    o_ref[...] = _hgrn_finish(oacc_ref[...], g_ref[...], gn_ref[...]).astype(o_ref.dtype)


def _hgrn_sample(p, lb, gnorm, state, layer, *, heads, name):
    bsz = p.shape[0]
    d = LANES
    p4 = p.reshape(bsz, 4, heads, d)
    part = lambda c: pl.BlockSpec((None, None, heads, d), lambda b: (b, c, 0, 0))
    o, s_new = pl.pallas_call(
        functools.partial(_hgrn_sample_body, heads=heads),
        grid=(bsz,),
        in_specs=[part(0), part(1), part(2), part(3),
                  pl.BlockSpec((heads, d), lambda b: (0, 0)),
                  pl.BlockSpec((1, d), lambda b: (0, 0)),
                  pl.BlockSpec((None, None, heads, d, d), lambda b: (layer, b, 0, 0, 0))],
        out_specs=[pl.BlockSpec((None, heads, d), lambda b: (b, 0, 0)),
                   pl.BlockSpec((None, heads, d, d), lambda b: (b, 0, 0, 0))],
        out_shape=[jax.ShapeDtypeStruct((bsz, heads, d), BF16),
                   jax.ShapeDtypeStruct((bsz, heads, d, d), F32)],
        scratch_shapes=[pltpu.VMEM((heads, d), F32)],
        compiler_params=_params(("parallel",), [((heads, d, d), F32)] * 2),
        name=name,
    )(p4, p4, p4, p4, lb.reshape(heads, d), gnorm.reshape(1, d), state)
    return o.reshape(bsz, heads * d), s_new


def _attend(q, k_ref, v_ref, heads, scale):
    outs = []
    for h in range(heads):
        sl = slice(h * LANES, (h + 1) * LANES)
        s = lax.dot_general(q[:, sl].astype(BF16), k_ref[:, sl].astype(BF16), (((1,), (1,)), ((), ())),
                            preferred_element_type=F32) * scale
        e = jnp.exp(s - jnp.max(s, axis=-1, keepdims=True))
        prob = e / jnp.sum(e, axis=-1, keepdims=True)
        outs.append(jnp.dot(prob.astype(BF16), v_ref[:, sl].astype(BF16), preferred_element_type=F32))
    return jnp.concatenate(outs, axis=1)


def _xattn_prompt_body(x_ref, gx_ref, gf_ref, wq_ref, wo_ref, k_ref, v_ref, y_ref, hf_ref, *, heads, scale):
    x = x_ref[...]
    h = _rms_rows(x, gx_ref[...]).astype(BF16)
    q = jnp.dot(h, wq_ref[...], preferred_element_type=F32)
    att = _attend(q, k_ref, v_ref, heads, scale).astype(BF16)
    y = x + jnp.dot(att, wo_ref[...], preferred_element_type=F32)
    y_ref[...] = y
    hf_ref[...] = _rms_rows(y, gf_ref[...]).astype(hf_ref.dtype)


def _xattn_prompt(x, gx, gf, wq, wo, mem_k, mem_v, *, batch, seq, tq, name):
    d = x.shape[1]
    xw = wq.shape[1]
    heads = xw // LANES
    mlen = mem_k.shape[1]
    nt = seq // tq
    rows = pl.BlockSpec((tq, d), lambda b, t: (b * nt + t, 0))
    vec = pl.BlockSpec((1, d), lambda b, t: (0, 0))
    mem = pl.BlockSpec((None, mlen, xw), lambda b, t: (b, 0, 0))
    return pl.pallas_call(
        functools.partial(_xattn_prompt_body, heads=heads, scale=LANES ** -0.5),
        grid=(batch, nt),
        in_specs=[rows, vec, vec,
                  pl.BlockSpec((d, xw), lambda b, t: (0, 0)),
                  pl.BlockSpec((xw, d), lambda b, t: (0, 0)),
                  mem, mem],
        out_specs=[rows, rows],
        out_shape=[jax.ShapeDtypeStruct(x.shape, F32), jax.ShapeDtypeStruct(x.shape, BF16)],
        compiler_params=_params(("parallel", "parallel"),
                                [((tq, d), F32)] * 2 + [((tq, d), BF16), ((d, xw), BF16), ((xw, d), BF16),
                                                        ((mlen, xw), F32), ((mlen, xw), F32)]),
        name=name,
    )(x, gx.reshape(1, d), gf.reshape(1, d), wq, wo, mem_k, mem_v)


def _xattn_sample_body(x_ref, gx_ref, gf_ref, wq_ref, wo_ref, k_ref, v_ref, y_ref, hf_ref, q_sc, att_sc,
                       *, heads, scale):
    b = pl.program_id(0)

    @pl.when(b == 0)
    def _():
        h = _rms_rows(x_ref[...], gx_ref[...]).astype(BF16)
        q_sc[...] = jnp.dot(h, wq_ref[...], preferred_element_type=F32)
        att_sc[...] = jnp.zeros_like(att_sc)

    att = _attend(q_sc[...], k_ref, v_ref, heads, scale)
    mine = lax.broadcasted_iota(jnp.int32, att.shape, 0) == b
    att_sc[...] += jnp.where(mine, att, 0.0)

    @pl.when(b == pl.num_programs(0) - 1)
    def _():
        y = x_ref[...] + jnp.dot(att_sc[...].astype(BF16), wo_ref[...], preferred_element_type=F32)
        y_ref[...] = y
        hf_ref[...] = _rms_rows(y, gf_ref[...]).astype(hf_ref.dtype)


def _xattn_sample(x, gx, gf, wq, wo, mem_k, mem_v, layer, name):
    bsz, d = x.shape
    xw = wq.shape[1]
    heads = xw // LANES
    mlen = mem_k.shape[2]
    mk = mem_k.reshape(mem_k.shape[0], bsz, mlen, xw)
    mv = mem_v.reshape(mem_v.shape[0], bsz, mlen, xw)
    rows = pl.BlockSpec((bsz, d), lambda b: (0, 0))
    vec = pl.BlockSpec((1, d), lambda b: (0, 0))
    mem = pl.BlockSpec((None, None, mlen, xw), lambda b: (layer, b, 0, 0))
    return pl.pallas_call(
        functools.partial(_xattn_sample_body, heads=heads, scale=LANES ** -0.5),
        grid=(bsz,),
        in_specs=[rows, vec, vec,
                  pl.BlockSpec((d, xw), lambda b: (0, 0)),
                  pl.BlockSpec((xw, d), lambda b: (0, 0)),
                  mem, mem],
        out_specs=[rows, rows],
        out_shape=[jax.ShapeDtypeStruct(x.shape, F32), jax.ShapeDtypeStruct(x.shape, BF16)],
        scratch_shapes=[pltpu.VMEM((bsz, xw), F32), pltpu.VMEM((bsz, xw), F32)],
        compiler_params=_params(("arbitrary",),
                                [((d, xw), BF16), ((xw, d), BF16), ((mlen, xw), F32), ((mlen, xw), F32)]),
        name=name,
    )(x, gx.reshape(1, d), gf.reshape(1, d), wq, wo, mk, mv)


def kernel(x_prompt, x_sample, mem_prompt, cache_sb_k, cache_sb_v, cache_mem_k, cache_mem_v, state_hgrn, page_table, norm_mix, norm_mem, norm_xattn, norm_ffn, norm_final, w_in_ab, w_out_ab, sb_bias, gmlp_norm, gmlp_ws, gmlp_bs, w_in_c, w_out_c, hgrn_lb, hgrn_gnorm, xattn_wq, xattn_wkv, xattn_wo, ffn_w13, ffn_w2):
    bp, tp, d = x_prompt.shape
    bd, ts, _ = x_sample.shape
    assert ts == 1, "the sample group carries one new token per sequence"
    depth = norm_mix.shape[0]
    a_heads = sb_bias.shape[1]
    a_width = a_heads * LANES
    groups = gmlp_ws.shape[1]
    b_width = groups * LANES
    c_heads = hgrn_lb.shape[1] // LANES
    c_width = c_heads * LANES
    mem_len = mem_prompt.shape[1]
    xw = xattn_wq.shape[2]
    hidden = ffn_w2.shape[1]
    assert a_width == b_width and w_in_ab.shape[2] == 3 * a_width + 2 * b_width

    xp = x_prompt.reshape(bp * tp, d)
    xs = x_sample.reshape(bd, d)
    mem = mem_prompt.reshape(bp * mem_len, d)

    lb_all = jnp.cumsum(jax.nn.softmax(hgrn_lb.astype(F32), axis=0), axis=0)
    lb_all = lb_all - lb_all[:1]

    tm_p, tm_s = 1024, bd
    ffn_tn = 256

    sbk_p, sbv_p, sbk_s, sbv_s, gv_s, hs_p, hs_s, mk_p, mv_p = [], [], [], [], [], [], [], [], []
    for l in range(depth):
        j = l // 2
        hp = _rmsnorm(xp, norm_mix[l], BF16, "norm_mix_p")
        hs = _rmsnorm(xs, norm_mix[l], BF16, "norm_mix_s")
        if l % 2 == 0:
            w_in = w_in_ab[j].astype(BF16)
            w_out = w_out_ab[j].astype(BF16)
            u_col = 3 * a_width // b_width
            pp = _matmul([hp], w_in, out_dtype=F32, tm=tm_p, tn=1024, name="in_ab_p")
            oa = _sb_prompt(pp, sb_bias[j], batch=bp, seq=tp, heads=a_heads, tq=256, name="sb_p")
            ob = _gmlp_prompt(pp, gmlp_norm[j], gmlp_ws[j], gmlp_bs[j], u_col=u_col, name="gmlp_p")
            xp = _matmul([oa, ob], w_out, out_dtype=F32, tm=tm_p, tn=1024, resid=xp, name="out_ab_p")
            sbk_p.append(pp[:, a_width:2 * a_width].reshape(bp, tp, a_heads, LANES))
            sbv_p.append(pp[:, 2 * a_width:3 * a_width].reshape(bp, tp, a_heads, LANES))
            ps = _matmul([hs], w_in, out_dtype=F32, tm=tm_s, tn=1024, name="in_ab_s")
            oa = _sb_sample(ps[:, :a_width].reshape(bd, a_heads, LANES), sb_bias[j], cache_sb_k, cache_sb_v,
                            page_table, j, "sb_s")
            ob, vn = _gmlp_sample(ps, gmlp_norm[j], gmlp_ws[j], gmlp_bs[j], u_col=u_col, name="gmlp_s")
            xs = _matmul([oa, ob], w_out, out_dtype=F32, tm=tm_s, tn=1024, resid=xs, name="out_ab_s")
            sbk_s.append(ps[:, a_width:2 * a_width].reshape(bd, ts, a_heads, LANES))
            sbv_s.append(ps[:, 2 * a_width:3 * a_width].reshape(bd, ts, a_heads, LANES))
            gv_s.append(vn.reshape(bd, ts, groups, LANES))
        else:
            w_in = w_in_c[j].astype(BF16)
            w_out = w_out_c[j].astype(BF16)
            pp = _matmul([hp], w_in, out_dtype=F32, tm=tm_p, tn=1024, name="in_c_p")
            oc, s_p = _hgrn_prompt(pp, lb_all[j], hgrn_gnorm[j], batch=bp, seq=tp, heads=c_heads, tb=256,
                                   name="hgrn_p")
            xp = _matmul([oc], w_out, out_dtype=F32, tm=tm_p, tn=1024, resid=xp, name="out_c_p")
            hs_p.append(s_p)
            ps = _matmul([hs], w_in, out_dtype=F32, tm=tm_s, tn=1024, name="in_c_s")
            oc, s_s = _hgrn_sample(ps, lb_all[j], hgrn_gnorm[j], state_hgrn, j, heads=c_heads, name="hgrn_s")
            xs = _matmul([oc], w_out, out_dtype=F32, tm=tm_s, tn=1024, resid=xs, name="out_c_s")
            hs_s.append(s_s)
        wq = xattn_wq[l].astype(BF16)
        wo = xattn_wo[l].astype(BF16)
        hm = _rmsnorm(mem, norm_mem[l], BF16, "norm_mem")
        kv = _matmul([hm], xattn_wkv[l].astype(BF16), out_dtype=F32, tm=bp * mem_len, tn=512, name="mem_kv")
        kp = kv[:, :xw].reshape(bp, mem_len, xw)
        vp = kv[:, xw:].reshape(bp, mem_len, xw)
        mk_p.append(kp.reshape(bp, mem_len, xw // LANES, LANES))
        mv_p.append(vp.reshape(bp, mem_len, xw // LANES, LANES))
        xp, hfp = _xattn_prompt(xp, norm_xattn[l], norm_ffn[l], wq, wo, kp, vp, batch=bp, seq=tp, tq=256,
                                name="xattn_p")
        xs, hfs = _xattn_sample(xs, norm_xattn[l], norm_ffn[l], wq, wo, cache_mem_k, cache_mem_v, l, "xattn_s")
        w13 = ffn_w13[l].astype(BF16)
        w2 = ffn_w2[l].astype(BF16)
        mid = _matmul_swiglu(hfp, w13, tm=tm_p, tn=ffn_tn, name="ffn13_p")
        xp = _matmul([mid], w2, out_dtype=F32, tm=512, tn=512, resid=xp, name="ffn2_p")
        mid = _matmul_swiglu(hfs, w13, tm=tm_s, tn=ffn_tn, name="ffn13_s")
        xs = _matmul([mid], w2, out_dtype=F32, tm=tm_s, tn=512, resid=xs, name="ffn2_s")
    y_prompt = _rmsnorm(xp, norm_final, F32, "norm_final_p").reshape(bp, tp, d)
    y_sample = _rmsnorm(xs, norm_final, F32, "norm_final_s").reshape(bd, ts, d)
    return (y_prompt, y_sample,
            jnp.stack(sbk_p, axis=1), jnp.stack(sbv_p, axis=1),
            jnp.stack(sbk_s, axis=1), jnp.stack(sbv_s, axis=1),
            jnp.stack(gv_s, axis=1),
            jnp.stack(hs_p, axis=0), jnp.stack(hs_s, axis=0),
            jnp.stack(mk_p, axis=0), jnp.stack(mv_p, axis=0))
```

```python
import functools

import numpy as np

import jax
import jax.numpy as jnp
from jax import lax
from jax.experimental import pallas as pl
from jax.experimental.pallas import tpu as pltpu

F32 = jnp.float32
BF16 = jnp.bfloat16
EPS = 1e-6
LANES = 128
SUBLANES = 8
VMEM_CAP = 60 * 1024 * 1024
VMEM_SLACK = 12 * 1024 * 1024
NT_DIMS = (((1,), (1,)), ((), ()))
TN_DIMS = (((0,), (0,)), ((), ()))


def _nbytes(shape, dtype):
    n = 1
    for s in shape:
        n *= s
    return n * jnp.dtype(dtype).itemsize


def _params(semantics, blocks):
    need = sum(_nbytes(b[0], b[1]) * (b[2] if len(b) > 2 else 2) for b in blocks) + VMEM_SLACK
    return pltpu.CompilerParams(dimension_semantics=semantics,
                                vmem_limit_bytes=int(min(need, VMEM_CAP)))


def _rms_rows(x, g):
    r = lax.rsqrt(jnp.mean(x * x, axis=-1, keepdims=True) + EPS)
    return (x * r) * g


def _bf16_pieces(x, terms):
    out = []
    for _ in range(terms):
        piece = x.astype(BF16)
        out.append(piece)
        x = x - piece.astype(F32)
    return out


def _neg_softplus(z):
    return -(jnp.maximum(z, 0.0) + jnp.log1p(jnp.exp(-jnp.abs(z))))


def _sum_over_later(log_keep, later, terms):
    acc = None
    for piece in _bf16_pieces(log_keep, terms):
        part = jnp.dot(piece, later, preferred_element_type=F32)
        acc = part if acc is None else acc + part
    return acc


def _norm_body(x_ref, g_ref, o_ref):
    o_ref[...] = _rms_rows(x_ref[...], g_ref[...]).astype(o_ref.dtype)


def _rmsnorm(x, g, out_dtype, name):
    m, d = x.shape
    tm = min(m, 256)
    return pl.pallas_call(
        _norm_body,
        grid=(m // tm,),
        in_specs=[pl.BlockSpec((tm, d), lambda i: (i, 0)),
                  pl.BlockSpec((1, d), lambda i: (0, 0))],
        out_specs=pl.BlockSpec((tm, d), lambda i: (i, 0)),
        out_shape=jax.ShapeDtypeStruct((m, d), out_dtype),
        compiler_params=_params(("parallel",), [((tm, d), F32), ((tm, d), out_dtype)]),
        name=name,
    )(x, g.reshape(1, d))


def _mm_body(*refs, k_sizes, has_resid):
    n_a = len(k_sizes)
    a_refs, w_ref, o_ref = refs[:n_a], refs[n_a], refs[-1]
    acc, off = None, 0
    for a_ref, kk in zip(a_refs, k_sizes):
        part = jnp.dot(a_ref[...], w_ref[off:off + kk, :], preferred_element_type=F32)
        acc = part if acc is None else acc + part
        off += kk
    if has_resid:
        acc = acc + refs[n_a + 1][...]
    o_ref[...] = acc.astype(o_ref.dtype)


def _matmul(a_list, w, *, out_dtype, tm, tn, resid=None, w_cols=None, w_single=False, name):
    m = a_list[0].shape[0]
    k_sizes = tuple(a.shape[1] for a in a_list)
    k = w.shape[0]
    first, n = w_cols if w_cols is not None else (0, w.shape[1])
    assert sum(k_sizes) == k and m % tm == 0 and n % tn == 0 and first % tn == 0
    j0 = first // tn
    in_specs = [pl.BlockSpec((tm, kk), lambda j, i: (i, 0)) for kk in k_sizes]
    w_mode = dict(pipeline_mode=pl.Buffered(1)) if w_single else {}
    in_specs.append(pl.BlockSpec((k, tn), lambda j, i: (0, j0 + j), **w_mode))
    blocks = [((tm, kk), BF16) for kk in k_sizes] + [((k, tn), BF16, 1 if w_single else 2), ((tm, tn), out_dtype)]
    args = list(a_list) + [w]
    if resid is not None:
        in_specs.append(pl.BlockSpec((tm, tn), lambda j, i: (i, j)))
        blocks.append(((tm, tn), F32))
        args.append(resid)
    return pl.pallas_call(
        functools.partial(_mm_body, k_sizes=k_sizes, has_resid=resid is not None),
        grid=(n // tn, m // tm),
        in_specs=in_specs,
        out_specs=pl.BlockSpec((tm, tn), lambda j, i: (i, j)),
        out_shape=jax.ShapeDtypeStruct((m, n), out_dtype),
        compiler_params=_params(("parallel", "parallel"), blocks),
        name=name,
    )(*args)


def _mm_heads_body(a_ref, w_ref, *rest, hb):
    o2_ref, o5_ref = rest[-2], rest[-1]
    acc = jnp.dot(a_ref[...], w_ref[...], preferred_element_type=F32)
    o2_ref[...] = acc.astype(o2_ref.dtype)
    for h in range(hb):
        o5_ref[:, h, :] = acc[:, h * LANES:(h + 1) * LANES]


def _matmul_heads(a, w, *, w_cols, batch, seq, slot, n_slots, stacked, tm, name):
    m, k = a.shape
    first, n = w_cols
    heads = n // LANES
    hb = SUBLANES
    tn = hb * LANES
    tps = seq // tm
    assert m == batch * seq and seq % tm == 0 and n % tn == 0 and first % tn == 0
    j0 = first // tn
    in_specs = [pl.BlockSpec((tm, k), lambda j, i: (i, 0)),
                pl.BlockSpec((k, tn), lambda j, i: (0, j0 + j))]
    args = [a, w]
    aliases = {}
    if stacked is not None:
        in_specs.append(pl.BlockSpec(memory_space=pl.ANY))
        args.append(stacked)
        aliases = {2: 1}
    return pl.pallas_call(
        functools.partial(_mm_heads_body, hb=hb),
        grid=(n // tn, m // tm),
        in_specs=in_specs,
        out_specs=[pl.BlockSpec((tm, tn), lambda j, i: (i, j)),
                   pl.BlockSpec((None, None, tm, hb, LANES), lambda j, i: (i // tps, slot, i % tps, j, 0))],
        out_shape=[jax.ShapeDtypeStruct((m, n), BF16),
                   jax.ShapeDtypeStruct((batch, n_slots, seq, heads, LANES), F32)],
        input_output_aliases=aliases,
        compiler_params=_params(("parallel", "parallel"),
                                [((tm, k), BF16), ((k, tn), BF16), ((tm, tn), BF16), ((tm, tn), F32)]),
        name=name,
    )(*args)


def _swiglu_body(a_ref, wa_ref, wb_ref, o_ref):
    a = a_ref[...]
    ga = jnp.dot(a, wa_ref[...], preferred_element_type=F32)
    gb = jnp.dot(a, wb_ref[...], preferred_element_type=F32)
    o_ref[...] = (jax.nn.silu(ga) * gb).astype(o_ref.dtype)


def _matmul_swiglu(a, w13, *, tm, tn, name):
    m, k = a.shape
    hidden = w13.shape[1] // 2
    assert m % tm == 0 and hidden % tn == 0
    nj = hidden // tn
    return pl.pallas_call(
        _swiglu_body,
        grid=(m // tm, nj),
        in_specs=[pl.BlockSpec((tm, k), lambda i, j: (i, 0)),
                  pl.BlockSpec((k, tn), lambda i, j: (0, j)),
                  pl.BlockSpec((k, tn), lambda i, j: (0, j + nj))],
        out_specs=pl.BlockSpec((tm, tn), lambda i, j: (i, j)),
        out_shape=jax.ShapeDtypeStruct((m, hidden), BF16),
        compiler_params=_params(("parallel", "parallel"),
                                [((tm, k), BF16), ((k, tn), BF16), ((k, tn), BF16), ((tm, tn), BF16)]),
        name=name,
    )(a, w13, w13)


def _sb_prompt_body(bias_ref, q_ref, k_ref, v_ref, o_ref, *, tq, hp, scale):
    qi = pl.program_id(2)
    d = LANES
    row = lax.broadcasted_iota(jnp.int32, (tq, tq), 0)
    col = lax.broadcasted_iota(jnp.int32, (tq, tq), 1)
    later = (row > col).astype(BF16)
    causal = col < row

    def block(kb, h, acc, carry, mask):
        hs = slice(h * d, (h + 1) * d)
        ks = k_ref[pl.ds(kb * tq, tq), hs]
        vs = v_ref[pl.ds(kb * tq, tq), hs]
        z = lax.dot_general(q_ref[:, hs], ks, NT_DIMS, preferred_element_type=F32)
        z = z * scale + bias_ref[h]
        log_keep = _neg_softplus(z)
        if mask:
            log_keep = jnp.where(causal, log_keep, 0.0)
        after = _sum_over_later(log_keep, later, 2)
        w = jnp.exp(log_keep + z + after + carry)
        if mask:
            w = jnp.where(causal, w, 0.0)
        acc = acc + jnp.dot(w.astype(BF16), vs, preferred_element_type=F32)
        carry = carry + after[:, :1] + log_keep[:, :1]
        return acc, carry

    state = []
    for h in range(hp):
        state.extend(block(qi, h, jnp.zeros((tq, d), F32), jnp.zeros((tq, 1), F32), True))

    def body(it, c):
        out = []
        for h in range(hp):
            out.extend(block(qi - 1 - it, h, c[2 * h], c[2 * h + 1], False))
        return tuple(out)

    state = lax.fori_loop(0, qi, body, tuple(state))
    for h in range(hp):
        o_ref[:, h * d:(h + 1) * d] = state[2 * h].astype(o_ref.dtype)


def _sb_prompt(q, k, v, bias, *, batch, seq, heads, tq, hp, name):
    d = LANES
    nq = seq // tq
    bias_rows = jnp.broadcast_to(bias.astype(F32)[:, None, None], (heads, 1, tq))
    q_spec = pl.BlockSpec((tq, hp * d), lambda b, h, i: (b * nq + i, h))
    kv_spec = pl.BlockSpec((seq, hp * d), lambda b, h, i: (b, h))
    return pl.pallas_call(
        functools.partial(_sb_prompt_body, tq=tq, hp=hp, scale=d ** -0.5),
        grid=(batch, heads // hp, nq),
        in_specs=[pl.BlockSpec((hp, 1, tq), lambda b, h, i: (h, 0, 0)), q_spec, kv_spec, kv_spec],
        out_specs=q_spec,
        out_shape=jax.ShapeDtypeStruct((batch * seq, heads * d), BF16),
        compiler_params=_params(("parallel", "parallel", "parallel"),
                                [((tq, hp * d), BF16)] * 2 + [((seq, hp * d), BF16)] * 2),
        name=name,
    )(bias_rows, q, k, v)


def _sb_sample_body(pt_ref, q_ref, bias_ref, k_ref, v_ref, o_ref, acc_ref, carry_ref, *, heads, scale):
    del pt_ref
    pg = pl.program_id(1)
    page = k_ref.shape[0]
    d = LANES

    @pl.when(pg == 0)
    def _():
        acc_ref[...] = jnp.zeros_like(acc_ref)
        carry_ref[...] = jnp.zeros_like(carry_ref)

    row = lax.broadcasted_iota(jnp.int32, (page, page), 0)
    col = lax.broadcasted_iota(jnp.int32, (page, page), 1)
    later = (row > col).astype(BF16)
    q = q_ref[...]
    kt = jnp.swapaxes(k_ref[...], 0, 1).astype(BF16)
    vt = jnp.swapaxes(v_ref[...], 0, 1).astype(BF16)
    hrow = lax.broadcasted_iota(jnp.int32, (heads, page), 0)
    z = jnp.zeros((heads, page), F32)
    for h in range(heads):
        zh = lax.dot_general(q, kt[h], NT_DIMS, preferred_element_type=F32)
        z = jnp.where(hrow == h, zh, z)
    z = z * scale + bias_ref[...]
    log_keep = _neg_softplus(z)
    after = _sum_over_later(log_keep, later, 3)
    w = jnp.exp(log_keep + z + after + carry_ref[...]).astype(BF16)
    carry_ref[...] += after[:, :1] + log_keep[:, :1]
    hrow_d = lax.broadcasted_iota(jnp.int32, (heads, d), 0)
    acc = acc_ref[...]
    for h in range(heads):
        oh = jnp.dot(w, vt[h], preferred_element_type=F32)
        acc = acc + jnp.where(hrow_d == h, oh, 0.0)
    acc_ref[...] = acc

    @pl.when(pg == pl.num_programs(1) - 1)
    def _():
        o_ref[...] = acc.astype(o_ref.dtype)


def _sb_sample(q, bias, cache_k, cache_v, page_table, layer, name):
    bsz, heads, d = q.shape
    page = cache_k.shape[2]
    n_pages = page_table.shape[1]
    bias_rows = jnp.broadcast_to(bias.astype(F32)[:, None], (heads, page))
    cache_spec = pl.BlockSpec((None, None, page, heads, d),
                              lambda b, p, pt: (pt[b, n_pages - 1 - p], layer, 0, 0, 0))
    grid_spec = pltpu.PrefetchScalarGridSpec(
        num_scalar_prefetch=1,
        grid=(bsz, n_pages),
        in_specs=[pl.BlockSpec((None, heads, d), lambda b, p, pt: (b, 0, 0)),
                  pl.BlockSpec((heads, page), lambda b, p, pt: (0, 0)),
                  cache_spec, cache_spec],
        out_specs=pl.BlockSpec((None, heads, d), lambda b, p, pt: (b, 0, 0)),
        scratch_shapes=[pltpu.VMEM((heads, d), F32), pltpu.VMEM((heads, 1), F32)],
    )
    out = pl.pallas_call(
        functools.partial(_sb_sample_body, heads=heads, scale=d ** -0.5),
        grid_spec=grid_spec,
        out_shape=jax.ShapeDtypeStruct((bsz, heads, d), BF16),
        compiler_params=_params(("parallel", "arbitrary"), [((page, heads, d), F32)] * 2),
        name=name,
    )(page_table, q.astype(BF16), bias_rows, cache_k, cache_v)
    return out.reshape(bsz, heads * d)


def _gmlp_prompt_body(u_ref, vb_ref, gn_ref, ws_ref, bst_ref, o_ref, *, groups):
    chunk = u_ref.shape[0]
    vn = _rms_rows(jax.nn.gelu(vb_ref[...]), gn_ref[...])
    row = lax.broadcasted_iota(jnp.int32, (chunk, chunk), 0)
    col = lax.broadcasted_iota(jnp.int32, (chunk, chunk), 1)
    for g in range(groups):
        sl = slice(g * LANES, (g + 1) * LANES)
        w = jnp.where(col <= row, ws_ref[g], 0.0).astype(BF16)
        s = jnp.dot(w, vn[:, sl].astype(BF16), preferred_element_type=F32) + bst_ref[:, g:g + 1]
        o_ref[:, sl] = (jax.nn.gelu(u_ref[:, sl]) * s).astype(o_ref.dtype)


def _gmlp_prompt(p, gnorm, ws, bs, *, u_col, name):
    groups, chunk, _ = ws.shape
    width = groups * LANES
    m = p.shape[0]
    return pl.pallas_call(
        functools.partial(_gmlp_prompt_body, groups=groups),
        grid=(m // chunk,),
        in_specs=[pl.BlockSpec((chunk, width), lambda i: (i, u_col)),
                  pl.BlockSpec((chunk, width), lambda i: (i, u_col + 1)),
                  pl.BlockSpec((1, width), lambda i: (0, 0)),
                  pl.BlockSpec((groups, chunk, chunk), lambda i: (0, 0, 0)),
                  pl.BlockSpec((chunk, groups), lambda i: (0, 0))],
        out_specs=pl.BlockSpec((chunk, width), lambda i: (i, 0)),
        out_shape=jax.ShapeDtypeStruct((m, width), BF16),
        compiler_params=_params(("parallel",),
                                [((chunk, width), F32)] * 2 + [((groups, chunk, chunk), F32), ((chunk, width), BF16)]),
        name=name,
    )(p, p, gnorm.reshape(1, width), ws, bs.T)


def _gmlp_sample_body(u_ref, vb_ref, gn_ref, w0_ref, b0_ref, o_ref, vn_ref):
    vn = _rms_rows(jax.nn.gelu(vb_ref[...]), gn_ref[...])
    vn_ref[...] = vn
    o_ref[...] = (jax.nn.gelu(u_ref[...]) * (w0_ref[...] * vn + b0_ref[...])).astype(o_ref.dtype)


def _gmlp_sample(p, gnorm, ws, bs, *, u_col, name):
    groups = ws.shape[0]
    width = groups * LANES
    m = p.shape[0]
    w0 = jnp.repeat(ws[:, 0, 0], LANES).reshape(1, width)
    b0 = jnp.repeat(bs[:, 0], LANES).reshape(1, width)
    row = lambda c: pl.BlockSpec((m, width), lambda i: (0, c))
    vec = pl.BlockSpec((1, width), lambda i: (0, 0))
    return pl.pallas_call(
        _gmlp_sample_body,
        grid=(1,),
        in_specs=[row(u_col), row(u_col + 1), vec, vec, vec],
        out_specs=[row(0), row(0)],
        out_shape=[jax.ShapeDtypeStruct((m, width), BF16), jax.ShapeDtypeStruct((m, width), F32)],
        compiler_params=_params(("arbitrary",), [((m, width), F32)] * 4),
        name=name,
    )(p, p, gnorm.reshape(1, width), w0, b0)


HGRN_LEVELS = 7


def _hgrn_gates(fpre, lb):
    t = jnp.log1p(jnp.exp(-jnp.abs(fpre)))
    log_sig = jnp.minimum(fpre, 0.0) - t
    log_sig_neg = jnp.minimum(-fpre, 0.0) - t
    a = jnp.log(lb)
    c = jnp.log1p(-lb) + log_sig
    logf = jnp.maximum(a, c) + jnp.log1p(jnp.exp(-jnp.abs(a - c)))
    return logf, (1.0 - lb) * jnp.exp(log_sig_neg)


def _hgrn_finish(o, g, gn):
    return _rms_rows(o, gn) * jax.nn.silu(g)


def _hgrn_tables():
    n = LANES
    t = np.arange(n)[:, None]
    r = np.arange(n)[None, :]
    blocks = []
    for level in range(HGRN_LEVELS):
        c = 1 << level
        mid = (t // (2 * c)) * (2 * c) + c
        second = (t % (2 * c)) >= c
        blocks.append(np.where(second, (r >= mid) & (r <= t), (r > t) & (r < mid)))
    blocks += [r <= t, r > t, np.ones((SUBLANES, n), bool)]
    ranges = np.concatenate(blocks, axis=0).astype(np.float32)
    diff = t ^ r
    level_of = np.where(t == r, -1, np.where(r < t, np.floor(np.log2(np.maximum(diff, 1))), HGRN_LEVELS))
    return (jnp.asarray(np.concatenate([ranges, ranges], axis=1), BF16),
            jnp.asarray(level_of.astype(np.int32)))


def _hgrn_prompt_body(q_ref, f_ref, i_ref, g_ref, lb_ref, gn_ref, rng_ref, lvl_ref, o_ref, s_ref, st_ref, *, tb):
    step = pl.program_id(2)
    n = LANES

    @pl.when(step == 0)
    def _():
        st_ref[...] = jnp.zeros_like(st_ref)

    lb = lb_ref[...]
    gn = gn_ref[...]
    level_of = lvl_ref[...]
    st = st_ref[...]
    tiles = [slice(i * n, (i + 1) * n) for i in range(tb // n)]
    logf_all, k_all = _hgrn_gates(f_ref[...], lb)
    rhs = jnp.concatenate([jnp.concatenate([piece[rs] for rs in tiles], axis=1)
                           for piece in _bf16_pieces(logf_all, 2)], axis=0)
    decay_all = jnp.exp(jnp.dot(rng_ref[...], rhs, preferred_element_type=F32))
    for rs in tiles:
        q = q_ref[rs, :]
        v = i_ref[rs, :]
        k = k_all[rs]
        decay = decay_all[:, rs]
        vb = v.astype(BF16)
        scores = jnp.zeros((n, n), F32)
        for level in range(HGRN_LEVELS):
            dl = decay[level * n:(level + 1) * n]
            pair = lax.dot_general((q * dl).astype(BF16), (k * dl).astype(BF16), NT_DIMS,
                                   preferred_element_type=F32)
            scores = jnp.where(level_of == level, pair, scores)
        same = lax.dot_general(q.astype(BF16), k.astype(BF16), NT_DIMS, preferred_element_type=F32)
        scores = jnp.where(level_of == -1, same, scores)
        from_start = decay[HGRN_LEVELS * n:(HGRN_LEVELS + 1) * n]
        to_end = decay[(HGRN_LEVELS + 1) * n:(HGRN_LEVELS + 2) * n]
        total = decay[(HGRN_LEVELS + 2) * n:(HGRN_LEVELS + 2) * n + 1]
        o = (jnp.dot(scores.astype(BF16), vb, preferred_element_type=F32)
             + lax.dot_general((q * from_start).astype(BF16), st.astype(BF16), NT_DIMS,
                               preferred_element_type=F32))
        st = st * total + lax.dot_general(vb, (k * to_end).astype(BF16), TN_DIMS, preferred_element_type=F32)
        o_ref[rs, :] = _hgrn_finish(o, g_ref[rs, :], gn).astype(o_ref.dtype)
    st_ref[...] = st

    @pl.when(step == pl.num_programs(2) - 1)
    def _():
        s_ref[...] = st.T


def _hgrn_prompt(p, lb, gnorm, *, batch, seq, heads, tb, name):
    d = LANES
    nt = seq // tb
    ranges, level_of = _hgrn_tables()
    col = lambda c: pl.BlockSpec((tb, d), lambda b, h, t: (b * nt + t, c * heads + h))
    return pl.pallas_call(
        functools.partial(_hgrn_prompt_body, tb=tb),
        grid=(batch, heads, nt),
        in_specs=[col(0), col(1), col(2), col(3),
                  pl.BlockSpec((None, 1, d), lambda b, h, t: (h, 0, 0)),
                  pl.BlockSpec((1, d), lambda b, h, t: (0, 0)),
                  pl.BlockSpec(ranges.shape, lambda b, h, t: (0, 0)),
                  pl.BlockSpec(level_of.shape, lambda b, h, t: (0, 0))],
        out_specs=[pl.BlockSpec((tb, d), lambda b, h, t: (b * nt + t, h)),
                   pl.BlockSpec((None, None, d, d), lambda b, h, t: (b, h, 0, 0))],
        out_shape=[jax.ShapeDtypeStruct((batch * seq, heads * d), BF16),
                   jax.ShapeDtypeStruct((batch, heads, d, d), F32)],
        scratch_shapes=[pltpu.VMEM((d, d), F32)],
        compiler_params=_params(("parallel", "parallel", "arbitrary"),
                                [((tb, d), F32)] * 5 + [((d, d), F32), (ranges.shape, BF16)]),
        name=name,
    )(p, p, p, p, lb.reshape(heads, 1, d), gnorm.reshape(1, d), ranges, level_of)


def _hgrn_sample_body(q_ref, f_ref, i_ref, g_ref, lb_ref, gn_ref, s_ref, o_ref, so_ref, oacc_ref, *, heads):
    d = LANES
    q = q_ref[...]
    v = i_ref[...]
    logf, k = _hgrn_gates(f_ref[...], lb_ref[...])
    f = jnp.exp(logf)
    eye = lax.broadcasted_iota(jnp.int32, (d, d), 0) == lax.broadcasted_iota(jnp.int32, (d, d), 1)

    def to_col(r):
        return jnp.sum(jnp.where(eye, r, 0.0), axis=1, keepdims=True)

    for h in range(heads):
        hs = slice(h, h + 1)
        s_new = to_col(f[hs]) * s_ref[h] + to_col(k[hs]) * v[hs]
        so_ref[h] = s_new
        oacc_ref[hs, :] = jnp.sum(to_col(q[hs]) * s_new, axis=0, keepdims=True)
    o_ref[...] = _hgrn_finish(oacc_ref[...], g_ref[...], gn_ref[...]).astype(o_ref.dtype)


def _hgrn_sample(p, lb, gnorm, state, layer, *, heads, name):
    bsz = p.shape[0]
    d = LANES
    p4 = p.reshape(bsz, 4, heads, d)
    part = lambda c: pl.BlockSpec((None, None, heads, d), lambda b: (b, c, 0, 0))
    o, s_new = pl.pallas_call(
        functools.partial(_hgrn_sample_body, heads=heads),
        grid=(bsz,),
        in_specs=[part(0), part(1), part(2), part(3),
                  pl.BlockSpec((heads, d), lambda b: (0, 0)),
                  pl.BlockSpec((1, d), lambda b: (0, 0)),
                  pl.BlockSpec((None, None, heads, d, d), lambda b: (layer, b, 0, 0, 0))],
        out_specs=[pl.BlockSpec((None, heads, d), lambda b: (b, 0, 0)),
                   pl.BlockSpec((None, heads, d, d), lambda b: (b, 0, 0, 0))],
        out_shape=[jax.ShapeDtypeStruct((bsz, heads, d), BF16),
                   jax.ShapeDtypeStruct((bsz, heads, d, d), F32)],
        scratch_shapes=[pltpu.VMEM((heads, d), F32)],
        compiler_params=_params(("parallel",), [((heads, d, d), F32)] * 2),
        name=name,
    )(p4, p4, p4, p4, lb.reshape(heads, d), gnorm.reshape(1, d), state)
    return o.reshape(bsz, heads * d), s_new


def _attend(q, k_ref, v_ref, heads, scale):
    outs = []
    for h in range(heads):
        sl = slice(h * LANES, (h + 1) * LANES)
        s = lax.dot_general(q[:, sl].astype(BF16), k_ref[:, sl].astype(BF16), NT_DIMS,
                            preferred_element_type=F32) * scale
        e = jnp.exp(s - jnp.max(s, axis=-1, keepdims=True))
        prob = e / jnp.sum(e, axis=-1, keepdims=True)
        outs.append(jnp.dot(prob.astype(BF16), v_ref[:, sl].astype(BF16), preferred_element_type=F32))
    return jnp.concatenate(outs, axis=1)


def _xattn_prompt_body(x_ref, gx_ref, gf_ref, wq_ref, wo_ref, k_ref, v_ref, y_ref, hf_ref, *, heads, scale):
    x = x_ref[...]
    h = _rms_rows(x, gx_ref[...]).astype(BF16)
    q = jnp.dot(h, wq_ref[...], preferred_element_type=F32)
    att = _attend(q, k_ref, v_ref, heads, scale).astype(BF16)
    y = x + jnp.dot(att, wo_ref[...], preferred_element_type=F32)
    y_ref[...] = y
    hf_ref[...] = _rms_rows(y, gf_ref[...]).astype(hf_ref.dtype)


def _xattn_prompt(x, gx, gf, wq, wo, mem_k, mem_v, *, batch, seq, tq, name):
    d = x.shape[1]
    xw = wq.shape[1]
    heads = xw // LANES
    mlen = mem_k.shape[1]
    nt = seq // tq
    rows = pl.BlockSpec((tq, d), lambda b, t: (b * nt + t, 0))
    vec = pl.BlockSpec((1, d), lambda b, t: (0, 0))
    mem = pl.BlockSpec((None, mlen, xw), lambda b, t: (b, 0, 0))
    return pl.pallas_call(
        functools.partial(_xattn_prompt_body, heads=heads, scale=LANES ** -0.5),
        grid=(batch, nt),
        in_specs=[rows, vec, vec,
                  pl.BlockSpec((d, xw), lambda b, t: (0, 0)),
                  pl.BlockSpec((xw, d), lambda b, t: (0, 0)),
                  mem, mem],
        out_specs=[rows, rows],
        out_shape=[jax.ShapeDtypeStruct(x.shape, F32), jax.ShapeDtypeStruct(x.shape, BF16)],
        compiler_params=_params(("parallel", "parallel"),
                                [((tq, d), F32)] * 2 + [((tq, d), BF16), ((d, xw), BF16), ((xw, d), BF16),
                                                        ((mlen, xw), F32), ((mlen, xw), F32)]),
        name=name,
    )(x, gx.reshape(1, d), gf.reshape(1, d), wq, wo, mem_k, mem_v)


def _xattn_sample_body(x_ref, gx_ref, gf_ref, wq_ref, wo_ref, k_ref, v_ref, y_ref, hf_ref, q_sc, att_sc,
                       *, heads, scale):
    b = pl.program_id(0)

    @pl.when(b == 0)
    def _():
        h = _rms_rows(x_ref[...], gx_ref[...]).astype(BF16)
        q_sc[...] = jnp.dot(h, wq_ref[...], preferred_element_type=F32)
        att_sc[...] = jnp.zeros_like(att_sc)

    att = _attend(q_sc[...], k_ref, v_ref, heads, scale)
    mine = lax.broadcasted_iota(jnp.int32, att.shape, 0) == b
    att_sc[...] += jnp.where(mine, att, 0.0)

    @pl.when(b == pl.num_programs(0) - 1)
    def _():
        y = x_ref[...] + jnp.dot(att_sc[...].astype(BF16), wo_ref[...], preferred_element_type=F32)
        y_ref[...] = y
        hf_ref[...] = _rms_rows(y, gf_ref[...]).astype(hf_ref.dtype)


def _xattn_sample(x, gx, gf, wq, wo, mem_k, mem_v, layer, name):
    bsz, d = x.shape
    xw = wq.shape[1]
    heads = xw // LANES
    mlen = mem_k.shape[2]
    mk = mem_k.reshape(mem_k.shape[0], bsz, mlen, xw)
    mv = mem_v.reshape(mem_v.shape[0], bsz, mlen, xw)
    rows = pl.BlockSpec((bsz, d), lambda b: (0, 0))
    vec = pl.BlockSpec((1, d), lambda b: (0, 0))
    mem = pl.BlockSpec((None, None, mlen, xw), lambda b: (layer, b, 0, 0))
    return pl.pallas_call(
        functools.partial(_xattn_sample_body, heads=heads, scale=LANES ** -0.5),
        grid=(bsz,),
        in_specs=[rows, vec, vec,
                  pl.BlockSpec((d, xw), lambda b: (0, 0)),
                  pl.BlockSpec((xw, d), lambda b: (0, 0)),
                  mem, mem],
        out_specs=[rows, rows],
        out_shape=[jax.ShapeDtypeStruct(x.shape, F32), jax.ShapeDtypeStruct(x.shape, BF16)],
        scratch_shapes=[pltpu.VMEM((bsz, xw), F32), pltpu.VMEM((bsz, xw), F32)],
        compiler_params=_params(("arbitrary",),
                                [((d, xw), BF16), ((xw, d), BF16), ((mlen, xw), F32), ((mlen, xw), F32)]),
        name=name,
    )(x, gx.reshape(1, d), gf.reshape(1, d), wq, wo, mk, mv)


def kernel(x_prompt, x_sample, mem_prompt, cache_sb_k, cache_sb_v, cache_mem_k, cache_mem_v, state_hgrn, page_table, norm_mix, norm_mem, norm_xattn, norm_ffn, norm_final, w_in_ab, w_out_ab, sb_bias, gmlp_norm, gmlp_ws, gmlp_bs, w_in_c, w_out_c, hgrn_lb, hgrn_gnorm, xattn_wq, xattn_wkv, xattn_wo, ffn_w13, ffn_w2):
    bp, tp, d = x_prompt.shape
    bd, ts, _ = x_sample.shape
    assert ts == 1, "the sample group carries one new token per sequence"
    depth = norm_mix.shape[0]
    n_even = w_in_ab.shape[0]
    a_heads = sb_bias.shape[1]
    a_width = a_heads * LANES
    groups = gmlp_ws.shape[1]
    b_width = groups * LANES
    c_heads = hgrn_lb.shape[1] // LANES
    mem_len = mem_prompt.shape[1]
    xw = xattn_wq.shape[2]
    assert a_width == b_width and w_in_ab.shape[2] == 3 * a_width + 2 * b_width

    xp = x_prompt.reshape(bp * tp, d)
    xs = x_sample.reshape(bd, d)
    mem = mem_prompt.reshape(bp * mem_len, d)

    lb_all = jnp.cumsum(jax.nn.softmax(hgrn_lb.astype(F32), axis=0), axis=0)
    lb_all = lb_all - lb_all[:1]

    tm_p, tm_s = 1024, bd
    ffn_tn = 256

    sbk_p, sbv_p = None, None
    sbk_s, sbv_s, gv_s, hs_p, hs_s, mk_p, mv_p = [], [], [], [], [], [], []
    for l in range(depth):
        j = l // 2
        hp = _rmsnorm(xp, norm_mix[l], BF16, "norm_mix_p")
        hs = _rmsnorm(xs, norm_mix[l], BF16, "norm_mix_s")
        if l % 2 == 0:
            w_in = w_in_ab[j].astype(BF16)
            w_out = w_out_ab[j].astype(BF16)
            qp = _matmul([hp], w_in, w_cols=(0, a_width), out_dtype=BF16, tm=tm_p, tn=1024, name="in_q_p")
            kp2, sbk_p = _matmul_heads(hp, w_in, w_cols=(a_width, a_width), batch=bp, seq=tp, slot=j,
                                       n_slots=n_even, stacked=sbk_p, tm=tm_p, name="in_k_p")
            vp2, sbv_p = _matmul_heads(hp, w_in, w_cols=(2 * a_width, a_width), batch=bp, seq=tp, slot=j,
                                       n_slots=n_even, stacked=sbv_p, tm=tm_p, name="in_v_p")
            uv = _matmul([hp], w_in, w_cols=(3 * a_width, 2 * b_width), out_dtype=F32, tm=tm_p, tn=1024,
                         name="in_uv_p")
            oa = _sb_prompt(qp, kp2, vp2, sb_bias[j], batch=bp, seq=tp, heads=a_heads, tq=512, hp=2, name="sb_p")
            ob = _gmlp_prompt(uv, gmlp_norm[j], gmlp_ws[j], gmlp_bs[j], u_col=0, name="gmlp_p")
            xp = _matmul([oa, ob], w_out, out_dtype=F32, tm=tm_p, tn=1024, resid=xp, name="out_ab_p")
            ps = _matmul([hs], w_in, out_dtype=F32, tm=tm_s, tn=1024, name="in_ab_s")
            oa = _sb_sample(ps[:, :a_width].reshape(bd, a_heads, LANES), sb_bias[j], cache_sb_k, cache_sb_v,
                            page_table, j, "sb_s")
            ob, vn = _gmlp_sample(ps, gmlp_norm[j], gmlp_ws[j], gmlp_bs[j], u_col=3 * a_width // b_width,
                                  name="gmlp_s")
            xs = _matmul([oa, ob], w_out, out_dtype=F32, tm=tm_s, tn=1024, resid=xs, name="out_ab_s")
            sbk_s.append(ps[:, a_width:2 * a_width].reshape(bd, ts, a_heads, LANES))
            sbv_s.append(ps[:, 2 * a_width:3 * a_width].reshape(bd, ts, a_heads, LANES))
            gv_s.append(vn.reshape(bd, ts, groups, LANES))
        else:
            w_in = w_in_c[j].astype(BF16)
            w_out = w_out_c[j].astype(BF16)
            pp = _matmul([hp], w_in, out_dtype=F32, tm=tm_p, tn=1024, name="in_c_p")
            oc, s_p = _hgrn_prompt(pp, lb_all[j], hgrn_gnorm[j], batch=bp, seq=tp, heads=c_heads, tb=512,
                                   name="hgrn_p")
            xp = _matmul([oc], w_out, out_dtype=F32, tm=tm_p, tn=1024, resid=xp, name="out_c_p")
            hs_p.append(s_p)
            ps = _matmul([hs], w_in, out_dtype=F32, tm=tm_s, tn=1024, name="in_c_s")
            oc, s_s = _hgrn_sample(ps, lb_all[j], hgrn_gnorm[j], state_hgrn, j, heads=c_heads, name="hgrn_s")
            xs = _matmul([oc], w_out, out_dtype=F32, tm=tm_s, tn=1024, resid=xs, name="out_c_s")
            hs_s.append(s_s)
        wq = xattn_wq[l].astype(BF16)
        wo = xattn_wo[l].astype(BF16)
        hm = _rmsnorm(mem, norm_mem[l], BF16, "norm_mem")
        kv = _matmul([hm], xattn_wkv[l].astype(BF16), out_dtype=F32, tm=bp * mem_len, tn=512, name="mem_kv")
        kp = kv[:, :xw].reshape(bp, mem_len, xw)
        vp = kv[:, xw:].reshape(bp, mem_len, xw)
        mk_p.append(kp.reshape(bp, mem_len, xw // LANES, LANES))
        mv_p.append(vp.reshape(bp, mem_len, xw // LANES, LANES))
        xp, hfp = _xattn_prompt(xp, norm_xattn[l], norm_ffn[l], wq, wo, kp, vp, batch=bp, seq=tp, tq=256,
                                name="xattn_p")
        xs, hfs = _xattn_sample(xs, norm_xattn[l], norm_ffn[l], wq, wo, cache_mem_k, cache_mem_v, l, "xattn_s")
        w13 = ffn_w13[l].astype(BF16)
        w2 = ffn_w2[l].astype(BF16)
        mid = _matmul_swiglu(hfp, w13, tm=tm_p, tn=ffn_tn, name="ffn13_p")
        xp = _matmul([mid], w2, out_dtype=F32, tm=512, tn=1024, resid=xp, w_single=True, name="ffn2_p")
        mid = _matmul_swiglu(hfs, w13, tm=tm_s, tn=ffn_tn, name="ffn13_s")
        xs = _matmul([mid], w2, out_dtype=F32, tm=tm_s, tn=512, resid=xs, name="ffn2_s")
    y_prompt = _rmsnorm(xp, norm_final, F32, "norm_final_p").reshape(bp, tp, d)
    y_sample = _rmsnorm(xs, norm_final, F32, "norm_final_s").reshape(bd, ts, d)
    return (y_prompt, y_sample, sbk_p, sbv_p,
            jnp.stack(sbk_s, axis=1), jnp.stack(sbv_s, axis=1),
            jnp.stack(gv_s, axis=1),
            jnp.stack(hs_p, axis=0), jnp.stack(hs_s, axis=0),
            jnp.stack(mk_p, axis=0), jnp.stack(mv_p, axis=0))
```

```python
import functools

import numpy as np

import jax
import jax.numpy as jnp
from jax import lax
from jax.experimental import pallas as pl
from jax.experimental.pallas import tpu as pltpu

F32 = jnp.float32
BF16 = jnp.bfloat16
EPS = 1e-6
LANES = 128
SUBLANES = 8
VMEM_CAP = 60 * 1024 * 1024
VMEM_SLACK = 12 * 1024 * 1024
NT_DIMS = (((1,), (1,)), ((), ()))
TN_DIMS = (((0,), (0,)), ((), ()))


def _nbytes(shape, dtype):
    n = 1
    for s in shape:
        n *= s
    return n * jnp.dtype(dtype).itemsize


def _params(semantics, blocks):
    need = sum(_nbytes(b[0], b[1]) * (b[2] if len(b) > 2 else 2) for b in blocks) + VMEM_SLACK
    return pltpu.CompilerParams(dimension_semantics=semantics,
                                vmem_limit_bytes=int(min(need, VMEM_CAP)))


def _rms_rows(x, g):
    r = lax.rsqrt(jnp.mean(x * x, axis=-1, keepdims=True) + EPS)
    return (x * r) * g


def _bf16_pieces(x, terms):
    out = []
    for _ in range(terms):
        piece = x.astype(BF16)
        out.append(piece)
        x = x - piece.astype(F32)
    return out


def _neg_softplus(z):
    return -(jnp.maximum(z, 0.0) + jnp.log1p(jnp.exp(-jnp.abs(z))))


def _sum_over_later(log_keep, later, terms):
    acc = None
    for piece in _bf16_pieces(log_keep, terms):
        part = jnp.dot(piece, later, preferred_element_type=F32)
        acc = part if acc is None else acc + part
    return acc


def _norm_body(x_ref, g_ref, o_ref):
    o_ref[...] = _rms_rows(x_ref[...], g_ref[...]).astype(o_ref.dtype)


def _rmsnorm(x, g, out_dtype, name):
    m, d = x.shape
    tm = min(m, 256)
    return pl.pallas_call(
        _norm_body,
        grid=(m // tm,),
        in_specs=[pl.BlockSpec((tm, d), lambda i: (i, 0)),
                  pl.BlockSpec((1, d), lambda i: (0, 0))],
        out_specs=pl.BlockSpec((tm, d), lambda i: (i, 0)),
        out_shape=jax.ShapeDtypeStruct((m, d), out_dtype),
        compiler_params=_params(("parallel",), [((tm, d), F32), ((tm, d), out_dtype)]),
        name=name,
    )(x, g.reshape(1, d))


def _w_spec(w, layer, k, tn, col_of, **mode):
    if w.ndim == 3:
        return pl.BlockSpec((None, k, tn), lambda j, i: (layer, 0, col_of(j)), **mode)
    return pl.BlockSpec((k, tn), lambda j, i: (0, col_of(j)), **mode)


def _cast_weights(w_refs, wb_refs):
    @pl.when(pl.program_id(1) == 0)
    def _():
        for w_ref, wb_ref in zip(w_refs, wb_refs):
            wb_ref[...] = w_ref[...].astype(BF16)


def _mm_body(*refs, k_sizes, has_resid, cast_w):
    n_a = len(k_sizes)
    a_refs, w_ref = refs[:n_a], refs[n_a]
    o_ref = refs[-2] if cast_w else refs[-1]
    if cast_w:
        _cast_weights([w_ref], [refs[-1]])
        w_ref = refs[-1]
    acc, off = None, 0
    for a_ref, kk in zip(a_refs, k_sizes):
        part = jnp.dot(a_ref[...], w_ref[off:off + kk, :], preferred_element_type=F32)
        acc = part if acc is None else acc + part
        off += kk
    if has_resid:
        acc = acc + refs[n_a + 1][...]
    o_ref[...] = acc.astype(o_ref.dtype)


def _matmul(a_list, w, *, out_dtype, tm, tn, layer=None, resid=None, w_cols=None, cast_w=False,
            w_single=False, name):
    m = a_list[0].shape[0]
    k_sizes = tuple(a.shape[1] for a in a_list)
    k = w.shape[-2]
    first, n = w_cols if w_cols is not None else (0, w.shape[-1])
    assert sum(k_sizes) == k and m % tm == 0 and n % tn == 0 and first % tn == 0
    assert not (cast_w and w_single)
    j0 = first // tn
    in_specs = [pl.BlockSpec((tm, kk), lambda j, i: (i, 0)) for kk in k_sizes]
    w_mode = dict(pipeline_mode=pl.Buffered(1)) if w_single else {}
    in_specs.append(_w_spec(w, layer, k, tn, lambda j: j0 + j, **w_mode))
    blocks = [((tm, kk), BF16) for kk in k_sizes] + [((k, tn), w.dtype, 1 if w_single else 2), ((tm, tn), out_dtype)]
    args = list(a_list) + [w]
    if resid is not None:
        in_specs.append(pl.BlockSpec((tm, tn), lambda j, i: (i, j)))
        blocks.append(((tm, tn), F32))
        args.append(resid)
    out_specs = [pl.BlockSpec((tm, tn), lambda j, i: (i, j))]
    out_shape = [jax.ShapeDtypeStruct((m, n), out_dtype)]
    if cast_w:
        out_specs.append(pl.BlockSpec((k, tn), lambda j, i: (0, j)))
        out_shape.append(jax.ShapeDtypeStruct((k, n), BF16))
        blocks.append(((k, tn), BF16))
    res = pl.pallas_call(
        functools.partial(_mm_body, k_sizes=k_sizes, has_resid=resid is not None, cast_w=cast_w),
        grid=(n // tn, m // tm),
        in_specs=in_specs,
        out_specs=out_specs,
        out_shape=out_shape,
        compiler_params=_params(("parallel", "arbitrary"), blocks),
        name=name,
    )(*args)
    return tuple(res) if cast_w else res[0]


def _mm_heads_body(a_ref, w_ref, *rest, hb):
    o2_ref, o5_ref = rest[-2], rest[-1]
    acc = jnp.dot(a_ref[...], w_ref[...], preferred_element_type=F32)
    o2_ref[...] = acc.astype(o2_ref.dtype)
    for h in range(hb):
        o5_ref[:, h, :] = acc[:, h * LANES:(h + 1) * LANES]


def _matmul_heads(a, w, *, layer, w_cols, batch, seq, slot, n_slots, stacked, tm, name):
    m, k = a.shape
    first, n = w_cols
    heads = n // LANES
    hb = SUBLANES
    tn = hb * LANES
    tps = seq // tm
    assert m == batch * seq and seq % tm == 0 and n % tn == 0 and first % tn == 0
    j0 = first // tn
    in_specs = [pl.BlockSpec((tm, k), lambda j, i: (i, 0)),
                _w_spec(w, layer, k, tn, lambda j: j0 + j)]
    args = [a, w]
    aliases = {}
    if stacked is not None:
        in_specs.append(pl.BlockSpec(memory_space=pl.ANY))
        args.append(stacked)
        aliases = {2: 1}
    return pl.pallas_call(
        functools.partial(_mm_heads_body, hb=hb),
        grid=(n // tn, m // tm),
        in_specs=in_specs,
        out_specs=[pl.BlockSpec((tm, tn), lambda j, i: (i, j)),
                   pl.BlockSpec((None, None, tm, hb, LANES), lambda j, i: (i // tps, slot, i % tps, j, 0))],
        out_shape=[jax.ShapeDtypeStruct((m, n), BF16),
                   jax.ShapeDtypeStruct((batch, n_slots, seq, heads, LANES), F32)],
        input_output_aliases=aliases,
        compiler_params=_params(("parallel", "parallel"),
                                [((tm, k), BF16), ((k, tn), BF16), ((tm, tn), BF16), ((tm, tn), F32)]),
        name=name,
    )(*args)


def _swiglu_body(a_ref, wa_ref, wb_ref, o_ref, *cast_refs):
    if cast_refs:
        _cast_weights([wa_ref, wb_ref], cast_refs)
        wa_ref, wb_ref = cast_refs
    a = a_ref[...]
    ga = jnp.dot(a, wa_ref[...], preferred_element_type=F32)
    gb = jnp.dot(a, wb_ref[...], preferred_element_type=F32)
    o_ref[...] = (jax.nn.silu(ga) * gb).astype(o_ref.dtype)


def _matmul_swiglu(a, w, *, tm, tn, layer=None, cast_w=False, name):
    m, k = a.shape
    hidden = w.shape[-1] // 2 if cast_w else w[0].shape[-1]
    assert m % tm == 0 and hidden % tn == 0
    nj = hidden // tn
    if cast_w:
        w_specs = [_w_spec(w, layer, k, tn, lambda j: j), _w_spec(w, layer, k, tn, lambda j: j + nj)]
        w_args = [w, w]
        w_dtype = F32
    else:
        w_specs = [_w_spec(w[0], None, k, tn, lambda j: j), _w_spec(w[1], None, k, tn, lambda j: j)]
        w_args = list(w)
        w_dtype = BF16
    out_specs = [pl.BlockSpec((tm, tn), lambda j, i: (i, j))]
    out_shape = [jax.ShapeDtypeStruct((m, hidden), BF16)]
    blocks = [((tm, k), BF16), ((k, tn), w_dtype), ((k, tn), w_dtype), ((tm, tn), BF16)]
    if cast_w:
        out_specs += [pl.BlockSpec((k, tn), lambda j, i: (0, j))] * 2
        out_shape += [jax.ShapeDtypeStruct((k, hidden), BF16)] * 2
        blocks += [((k, tn), BF16)] * 2
    res = pl.pallas_call(
        _swiglu_body,
        grid=(nj, m // tm),
        in_specs=[pl.BlockSpec((tm, k), lambda j, i: (i, 0))] + w_specs,
        out_specs=out_specs,
        out_shape=out_shape,
        compiler_params=_params(("parallel", "arbitrary"), blocks),
        name=name,
    )(a, *w_args)
    return (res[0], (res[1], res[2])) if cast_w else res[0]


def _sb_prompt_body(bias_ref, q_ref, k_ref, v_ref, o_ref, *, tq, hp, scale):
    qi = pl.program_id(2)
    d = LANES
    row = lax.broadcasted_iota(jnp.int32, (tq, tq), 0)
    col = lax.broadcasted_iota(jnp.int32, (tq, tq), 1)
    later = (row > col).astype(BF16)
    causal = col < row

    def block(kb, h, acc, carry, mask):
        hs = slice(h * d, (h + 1) * d)
        ks = k_ref[pl.ds(kb * tq, tq), hs]
        vs = v_ref[pl.ds(kb * tq, tq), hs]
        z = lax.dot_general(q_ref[:, hs], ks, NT_DIMS, preferred_element_type=F32)
        z = z * scale + bias_ref[h]
        log_keep = _neg_softplus(z)
        if mask:
            log_keep = jnp.where(causal, log_keep, 0.0)
        after = _sum_over_later(log_keep, later, 2)
        w = jnp.exp(log_keep + z + after + carry)
        if mask:
            w = jnp.where(causal, w, 0.0)
        acc = acc + jnp.dot(w.astype(BF16), vs, preferred_element_type=F32)
        carry = carry + after[:, :1] + log_keep[:, :1]
        return acc, carry

    state = []
    for h in range(hp):
        state.extend(block(qi, h, jnp.zeros((tq, d), F32), jnp.zeros((tq, 1), F32), True))

    def body(it, c):
        out = []
        for h in range(hp):
            out.extend(block(qi - 1 - it, h, c[2 * h], c[2 * h + 1], False))
        return tuple(out)

    state = lax.fori_loop(0, qi, body, tuple(state))
    for h in range(hp):
        o_ref[:, h * d:(h + 1) * d] = state[2 * h].astype(o_ref.dtype)


def _sb_prompt(q, k, v, bias, *, batch, seq, heads, tq, hp, name):
    d = LANES
    nq = seq // tq
    bias_rows = jnp.broadcast_to(bias.astype(F32)[:, None, None], (heads, 1, tq))
    q_spec = pl.BlockSpec((tq, hp * d), lambda b, h, i: (b * nq + i, h))
    kv_spec = pl.BlockSpec((seq, hp * d), lambda b, h, i: (b, h))
    return pl.pallas_call(
        functools.partial(_sb_prompt_body, tq=tq, hp=hp, scale=d ** -0.5),
        grid=(batch, heads // hp, nq),
        in_specs=[pl.BlockSpec((hp, 1, tq), lambda b, h, i: (h, 0, 0)), q_spec, kv_spec, kv_spec],
        out_specs=q_spec,
        out_shape=jax.ShapeDtypeStruct((batch * seq, heads * d), BF16),
        compiler_params=_params(("parallel", "parallel", "parallel"),
                                [((tq, hp * d), BF16)] * 2 + [((seq, hp * d), BF16)] * 2),
        name=name,
    )(bias_rows, q, k, v)


def _sb_sample_body(pt_ref, q_ref, bias_ref, *refs, heads, pps, scale):
    del pt_ref
    k_refs, v_refs = refs[:pps], refs[pps:2 * pps]
    o_ref, acc_ref, carry_ref = refs[2 * pps:]
    pg = pl.program_id(1)
    page = k_refs[0].shape[0]
    d = LANES

    @pl.when(pg == 0)
    def _():
        acc_ref[...] = jnp.zeros_like(acc_ref)
        carry_ref[...] = jnp.zeros_like(carry_ref)

    row = lax.broadcasted_iota(jnp.int32, (page, page), 0)
    col = lax.broadcasted_iota(jnp.int32, (page, page), 1)
    later = (row > col).astype(BF16)
    hrow = lax.broadcasted_iota(jnp.int32, (heads, page), 0)
    hrow_d = lax.broadcasted_iota(jnp.int32, (heads, d), 0)
    q = q_ref[...]
    bias = bias_ref[...]
    acc = acc_ref[...]
    carry = carry_ref[...]
    for k_ref, v_ref in zip(k_refs, v_refs):
        kt = jnp.swapaxes(k_ref[...], 0, 1).astype(BF16)
        vt = jnp.swapaxes(v_ref[...], 0, 1).astype(BF16)
        z = jnp.zeros((heads, page), F32)
        for h in range(heads):
            zh = lax.dot_general(q, kt[h], NT_DIMS, preferred_element_type=F32)
            z = jnp.where(hrow == h, zh, z)
        z = z * scale + bias
        log_keep = _neg_softplus(z)
        after = _sum_over_later(log_keep, later, 3)
        w = jnp.exp(log_keep + z + after + carry).astype(BF16)
        carry = carry + after[:, :1] + log_keep[:, :1]
        for h in range(heads):
            oh = jnp.dot(w, vt[h], preferred_element_type=F32)
            acc = acc + jnp.where(hrow_d == h, oh, 0.0)
    acc_ref[...] = acc
    carry_ref[...] = carry

    @pl.when(pg == pl.num_programs(1) - 1)
    def _():
        o_ref[...] = acc.astype(o_ref.dtype)


def _sb_sample(q, bias, cache_k, cache_v, page_table, layer, *, pps, name):
    bsz, heads, d = q.shape
    page = cache_k.shape[2]
    n_pages = page_table.shape[1]
    assert n_pages % pps == 0
    bias_rows = jnp.broadcast_to(bias.astype(F32)[:, None], (heads, page))

    def cache_spec(r):
        return pl.BlockSpec((None, None, page, heads, d),
                            lambda b, p, pt: (pt[b, n_pages - 1 - (p * pps + r)], layer, 0, 0, 0))

    grid_spec = pltpu.PrefetchScalarGridSpec(
        num_scalar_prefetch=1,
        grid=(bsz, n_pages // pps),
        in_specs=[pl.BlockSpec((None, heads, d), lambda b, p, pt: (b, 0, 0)),
                  pl.BlockSpec((heads, page), lambda b, p, pt: (0, 0))]
                 + [cache_spec(r) for r in range(pps)] * 2,
        out_specs=pl.BlockSpec((None, heads, d), lambda b, p, pt: (b, 0, 0)),
        scratch_shapes=[pltpu.VMEM((heads, d), F32), pltpu.VMEM((heads, 1), F32)],
    )
    out = pl.pallas_call(
        functools.partial(_sb_sample_body, heads=heads, pps=pps, scale=d ** -0.5),
        grid_spec=grid_spec,
        out_shape=jax.ShapeDtypeStruct((bsz, heads, d), BF16),
        compiler_params=_params(("parallel", "arbitrary"), [((page, heads, d), F32)] * (2 * pps)),
        name=name,
    )(page_table, q.astype(BF16), bias_rows, *([cache_k] * pps), *([cache_v] * pps))
    return out.reshape(bsz, heads * d)


def _gmlp_prompt_body(u_ref, vb_ref, gn_ref, ws_ref, bst_ref, o_ref, *, groups):
    chunk = u_ref.shape[0]
    vn = _rms_rows(jax.nn.gelu(vb_ref[...]), gn_ref[...])
    row = lax.broadcasted_iota(jnp.int32, (chunk, chunk), 0)
    col = lax.broadcasted_iota(jnp.int32, (chunk, chunk), 1)
    for g in range(groups):
        sl = slice(g * LANES, (g + 1) * LANES)
        w = jnp.where(col <= row, ws_ref[g], 0.0).astype(BF16)
        s = jnp.dot(w, vn[:, sl].astype(BF16), preferred_element_type=F32) + bst_ref[:, g:g + 1]
        o_ref[:, sl] = (jax.nn.gelu(u_ref[:, sl]) * s).astype(o_ref.dtype)


def _gmlp_prompt(p, gnorm, ws, bs, *, u_col, name):
    groups, chunk, _ = ws.shape
    width = groups * LANES
    m = p.shape[0]
    return pl.pallas_call(
        functools.partial(_gmlp_prompt_body, groups=groups),
        grid=(m // chunk,),
        in_specs=[pl.BlockSpec((chunk, width), lambda i: (i, u_col)),
                  pl.BlockSpec((chunk, width), lambda i: (i, u_col + 1)),
                  pl.BlockSpec((1, width), lambda i: (0, 0)),
                  pl.BlockSpec((groups, chunk, chunk), lambda i: (0, 0, 0)),
                  pl.BlockSpec((chunk, groups), lambda i: (0, 0))],
        out_specs=pl.BlockSpec((chunk, width), lambda i: (i, 0)),
        out_shape=jax.ShapeDtypeStruct((m, width), BF16),
        compiler_params=_params(("parallel",),
                                [((chunk, width), F32)] * 2 + [((groups, chunk, chunk), F32), ((chunk, width), BF16)]),
        name=name,
    )(p, p, gnorm.reshape(1, width), ws, bs.T)


def _gmlp_sample_body(u_ref, vb_ref, gn_ref, w0_ref, b0_ref, o_ref, vn_ref):
    vn = _rms_rows(jax.nn.gelu(vb_ref[...]), gn_ref[...])
    vn_ref[...] = vn
    o_ref[...] = (jax.nn.gelu(u_ref[...]) * (w0_ref[...] * vn + b0_ref[...])).astype(o_ref.dtype)


def _gmlp_sample(p, gnorm, ws, bs, *, u_col, name):
    groups = ws.shape[0]
    width = groups * LANES
    m = p.shape[0]
    w0 = jnp.repeat(ws[:, 0, 0], LANES).reshape(1, width)
    b0 = jnp.repeat(bs[:, 0], LANES).reshape(1, width)
    row = lambda c: pl.BlockSpec((m, width), lambda i: (0, c))
    vec = pl.BlockSpec((1, width), lambda i: (0, 0))
    return pl.pallas_call(
        _gmlp_sample_body,
        grid=(1,),
        in_specs=[row(u_col), row(u_col + 1), vec, vec, vec],
        out_specs=[row(0), row(0)],
        out_shape=[jax.ShapeDtypeStruct((m, width), BF16), jax.ShapeDtypeStruct((m, width), F32)],
        compiler_params=_params(("arbitrary",), [((m, width), F32)] * 4),
        name=name,
    )(p, p, gnorm.reshape(1, width), w0, b0)


HGRN_LEVELS = 7


def _hgrn_gates(fpre, lb):
    t = jnp.log1p(jnp.exp(-jnp.abs(fpre)))
    log_sig = jnp.minimum(fpre, 0.0) - t
    log_sig_neg = jnp.minimum(-fpre, 0.0) - t
    a = jnp.log(lb)
    c = jnp.log1p(-lb) + log_sig
    logf = jnp.maximum(a, c) + jnp.log1p(jnp.exp(-jnp.abs(a - c)))
    return logf, (1.0 - lb) * jnp.exp(log_sig_neg)


def _hgrn_finish(o, g, gn):
    return _rms_rows(o, gn) * jax.nn.silu(g)


def _hgrn_tables():
    n = LANES
    t = np.arange(n)[:, None]
    r = np.arange(n)[None, :]
    blocks = []
    for level in range(HGRN_LEVELS):
        c = 1 << level
        mid = (t // (2 * c)) * (2 * c) + c
        second = (t % (2 * c)) >= c
        blocks.append(np.where(second, (r >= mid) & (r <= t), (r > t) & (r < mid)))
    blocks += [r <= t, r > t, np.ones((SUBLANES, n), bool)]
    ranges = np.concatenate(blocks, axis=0).astype(np.float32)
    diff = t ^ r
    level_of = np.where(t == r, -1, np.where(r < t, np.floor(np.log2(np.maximum(diff, 1))), HGRN_LEVELS))
    return (jnp.asarray(np.concatenate([ranges, ranges], axis=1), BF16),
            jnp.asarray(level_of.astype(np.int32)))


def _hgrn_prompt_body(q_ref, f_ref, i_ref, g_ref, lb_ref, gn_ref, rng_ref, lvl_ref, o_ref, s_ref, st_ref, *, tb):
    step = pl.program_id(2)
    n = LANES

    @pl.when(step == 0)
    def _():
        st_ref[...] = jnp.zeros_like(st_ref)

    lb = lb_ref[...]
    gn = gn_ref[...]
    level_of = lvl_ref[...]
    st = st_ref[...]
    tiles = [slice(i * n, (i + 1) * n) for i in range(tb // n)]
    logf_all, k_all = _hgrn_gates(f_ref[...], lb)
    rhs = jnp.concatenate([jnp.concatenate([piece[rs] for rs in tiles], axis=1)
                           for piece in _bf16_pieces(logf_all, 2)], axis=0)
    decay_all = jnp.exp(jnp.dot(rng_ref[...], rhs, preferred_element_type=F32))
    for rs in tiles:
        q = q_ref[rs, :]
        v = i_ref[rs, :]
        k = k_all[rs]
        decay = decay_all[:, rs]
        vb = v.astype(BF16)
        scores = jnp.zeros((n, n), F32)
        for level in range(HGRN_LEVELS):
            dl = decay[level * n:(level + 1) * n]
            pair = lax.dot_general((q * dl).astype(BF16), (k * dl).astype(BF16), NT_DIMS,
                                   preferred_element_type=F32)
            scores = jnp.where(level_of == level, pair, scores)
        same = lax.dot_general(q.astype(BF16), k.astype(BF16), NT_DIMS, preferred_element_type=F32)
        scores = jnp.where(level_of == -1, same, scores)
        from_start = decay[HGRN_LEVELS * n:(HGRN_LEVELS + 1) * n]
        to_end = decay[(HGRN_LEVELS + 1) * n:(HGRN_LEVELS + 2) * n]
        total = decay[(HGRN_LEVELS + 2) * n:(HGRN_LEVELS + 2) * n + 1]
        o = (jnp.dot(scores.astype(BF16), vb, preferred_element_type=F32)
             + lax.dot_general((q * from_start).astype(BF16), st.astype(BF16), NT_DIMS,
                               preferred_element_type=F32))
        st = st * total + lax.dot_general(vb, (k * to_end).astype(BF16), TN_DIMS, preferred_element_type=F32)
        o_ref[rs, :] = _hgrn_finish(o, g_ref[rs, :], gn).astype(o_ref.dtype)
    st_ref[...] = st

    @pl.when(step == pl.num_programs(2) - 1)
    def _():
        s_ref[...] = st.T


def _hgrn_prompt(p, lb, gnorm, *, batch, seq, heads, tb, name):
    d = LANES
    nt = seq // tb
    ranges, level_of = _hgrn_tables()
    col = lambda c: pl.BlockSpec((tb, d), lambda b, h, t: (b * nt + t, c * heads + h))
    return pl.pallas_call(
        functools.partial(_hgrn_prompt_body, tb=tb),
        grid=(batch, heads, nt),
        in_specs=[col(0), col(1), col(2), col(3),
                  pl.BlockSpec((None, 1, d), lambda b, h, t: (h, 0, 0)),
                  pl.BlockSpec((1, d), lambda b, h, t: (0, 0)),
                  pl.BlockSpec(ranges.shape, lambda b, h, t: (0, 0)),
                  pl.BlockSpec(level_of.shape, lambda b, h, t: (0, 0))],
        out_specs=[pl.BlockSpec((tb, d), lambda b, h, t: (b * nt + t, h)),
                   pl.BlockSpec((None, None, d, d), lambda b, h, t: (b, h, 0, 0))],
        out_shape=[jax.ShapeDtypeStruct((batch * seq, heads * d), BF16),
                   jax.ShapeDtypeStruct((batch, heads, d, d), F32)],
        scratch_shapes=[pltpu.VMEM((d, d), F32)],
        compiler_params=_params(("parallel", "parallel", "arbitrary"),
                                [((tb, d), F32)] * 5 + [((d, d), F32), (ranges.shape, BF16)]),
        name=name,
    )(p, p, p, p, lb.reshape(heads, 1, d), gnorm.reshape(1, d), ranges, level_of)


def _hgrn_sample_body(q_ref, f_ref, i_ref, g_ref, lb_ref, gn_ref, s_ref, o_ref, so_ref, oacc_ref, *, heads):
    d = LANES
    q = q_ref[...]
    v = i_ref[...]
    logf, k = _hgrn_gates(f_ref[...], lb_ref[...])
    f = jnp.exp(logf)
    eye = lax.broadcasted_iota(jnp.int32, (d, d), 0) == lax.broadcasted_iota(jnp.int32, (d, d), 1)

    def to_col(r):
        return jnp.sum(jnp.where(eye, r, 0.0), axis=1, keepdims=True)

    for h in range(heads):
        hs = slice(h, h + 1)
        s_new = to_col(f[hs]) * s_ref[h] + to_col(k[hs]) * v[hs]
        so_ref[h] = s_new
        oacc_ref[hs, :] = jnp.sum(to_col(q[hs]) * s_new, axis=0, keepdims=True)
    o_ref[...] = _hgrn_finish(oacc_ref[...], g_ref[...], gn_ref[...]).astype(o_ref.dtype)


def _hgrn_sample(p, lb, gnorm, state, layer, *, heads, name):
    bsz = p.shape[0]
    d = LANES
    p4 = p.reshape(bsz, 4, heads, d)
    part = lambda c: pl.BlockSpec((None, None, heads, d), lambda b: (b, c, 0, 0))
    o, s_new = pl.pallas_call(
        functools.partial(_hgrn_sample_body, heads=heads),
        grid=(bsz,),
        in_specs=[part(0), part(1), part(2), part(3),
                  pl.BlockSpec((heads, d), lambda b: (0, 0)),
                  pl.BlockSpec((1, d), lambda b: (0, 0)),
                  pl.BlockSpec((None, None, heads, d, d), lambda b: (layer, b, 0, 0, 0))],
        out_specs=[pl.BlockSpec((None, heads, d), lambda b: (b, 0, 0)),
                   pl.BlockSpec((None, heads, d, d), lambda b: (b, 0, 0, 0))],
        out_shape=[jax.ShapeDtypeStruct((bsz, heads, d), BF16),
                   jax.ShapeDtypeStruct((bsz, heads, d, d), F32)],
        scratch_shapes=[pltpu.VMEM((heads, d), F32)],
        compiler_params=_params(("parallel",), [((heads, d, d), F32)] * 2),
        name=name,
    )(p4, p4, p4, p4, lb.reshape(heads, d), gnorm.reshape(1, d), state)
    return o.reshape(bsz, heads * d), s_new


def _attend(q, k_ref, v_ref, heads, scale):
    outs = []
    for h in range(heads):
        sl = slice(h * LANES, (h + 1) * LANES)
        s = lax.dot_general(q[:, sl].astype(BF16), k_ref[:, sl].astype(BF16), NT_DIMS,
                            preferred_element_type=F32) * scale
        e = jnp.exp(s - jnp.max(s, axis=-1, keepdims=True))
        prob = e / jnp.sum(e, axis=-1, keepdims=True)
        outs.append(jnp.dot(prob.astype(BF16), v_ref[:, sl].astype(BF16), preferred_element_type=F32))
    return jnp.concatenate(outs, axis=1)


def _xattn_prompt_body(x_ref, gx_ref, gf_ref, wq_ref, wo_ref, k_ref, v_ref, y_ref, hf_ref, *, heads, scale):
    x = x_ref[...]
    h = _rms_rows(x, gx_ref[...]).astype(BF16)
    q = jnp.dot(h, wq_ref[...], preferred_element_type=F32)
    att = _attend(q, k_ref, v_ref, heads, scale).astype(BF16)
    y = x + jnp.dot(att, wo_ref[...], preferred_element_type=F32)
    y_ref[...] = y
    hf_ref[...] = _rms_rows(y, gf_ref[...]).astype(hf_ref.dtype)


def _xattn_prompt(x, gx, gf, wq, wo, layer, mem_k, mem_v, *, batch, seq, tq, name):
    d = x.shape[1]
    xw = wq.shape[2]
    heads = xw // LANES
    mlen = mem_k.shape[1]
    nt = seq // tq
    rows = pl.BlockSpec((tq, d), lambda b, t: (b * nt + t, 0))
    vec = pl.BlockSpec((1, d), lambda b, t: (0, 0))
    mem = pl.BlockSpec((None, mlen, xw), lambda b, t: (b, 0, 0))
    return pl.pallas_call(
        functools.partial(_xattn_prompt_body, heads=heads, scale=LANES ** -0.5),
        grid=(batch, nt),
        in_specs=[rows, vec, vec,
                  pl.BlockSpec((None, d, xw), lambda b, t: (layer, 0, 0)),
                  pl.BlockSpec((None, xw, d), lambda b, t: (layer, 0, 0)),
                  mem, mem],
        out_specs=[rows, rows],
        out_shape=[jax.ShapeDtypeStruct(x.shape, F32), jax.ShapeDtypeStruct(x.shape, BF16)],
        compiler_params=_params(("parallel", "parallel"),
                                [((tq, d), F32)] * 2 + [((tq, d), BF16), ((d, xw), BF16), ((xw, d), BF16),
                                                        ((mlen, xw), F32), ((mlen, xw), F32)]),
        name=name,
    )(x, gx.reshape(1, d), gf.reshape(1, d), wq, wo, mem_k, mem_v)


def _xattn_sample_body(x_ref, gx_ref, gf_ref, wq_ref, wo_ref, k_ref, v_ref, y_ref, hf_ref, q_sc, att_sc,
                       *, heads, scale):
    b = pl.program_id(0)

    @pl.when(b == 0)
    def _():
        h = _rms_rows(x_ref[...], gx_ref[...]).astype(BF16)
        q_sc[...] = jnp.dot(h, wq_ref[...], preferred_element_type=F32)
        att_sc[...] = jnp.zeros_like(att_sc)

    att = _attend(q_sc[...], k_ref, v_ref, heads, scale)
    mine = lax.broadcasted_iota(jnp.int32, att.shape, 0) == b
    att_sc[...] += jnp.where(mine, att, 0.0)

    @pl.when(b == pl.num_programs(0) - 1)
    def _():
        y = x_ref[...] + jnp.dot(att_sc[...].astype(BF16), wo_ref[...], preferred_element_type=F32)
        y_ref[...] = y
        hf_ref[...] = _rms_rows(y, gf_ref[...]).astype(hf_ref.dtype)


def _xattn_sample(x, gx, gf, wq, wo, mem_k, mem_v, layer, name):
    bsz, d = x.shape
    xw = wq.shape[2]
    heads = xw // LANES
    mlen = mem_k.shape[2]
    mk = mem_k.reshape(mem_k.shape[0], bsz, mlen, xw)
    mv = mem_v.reshape(mem_v.shape[0], bsz, mlen, xw)
    rows = pl.BlockSpec((bsz, d), lambda b: (0, 0))
    vec = pl.BlockSpec((1, d), lambda b: (0, 0))
    mem = pl.BlockSpec((None, None, mlen, xw), lambda b: (layer, b, 0, 0))
    return pl.pallas_call(
        functools.partial(_xattn_sample_body, heads=heads, scale=LANES ** -0.5),
        grid=(bsz,),
        in_specs=[rows, vec, vec,
                  pl.BlockSpec((None, d, xw), lambda b: (layer, 0, 0)),
                  pl.BlockSpec((None, xw, d), lambda b: (layer, 0, 0)),
                  mem, mem],
        out_specs=[rows, rows],
        out_shape=[jax.ShapeDtypeStruct(x.shape, F32), jax.ShapeDtypeStruct(x.shape, BF16)],
        scratch_shapes=[pltpu.VMEM((bsz, xw), F32), pltpu.VMEM((bsz, xw), F32)],
        compiler_params=_params(("arbitrary",),
                                [((d, xw), BF16), ((xw, d), BF16), ((mlen, xw), F32), ((mlen, xw), F32)]),
        name=name,
    )(x, gx.reshape(1, d), gf.reshape(1, d), wq, wo, mk, mv)


def kernel(x_prompt, x_sample, mem_prompt, cache_sb_k, cache_sb_v, cache_mem_k, cache_mem_v, state_hgrn, page_table, norm_mix, norm_mem, norm_xattn, norm_ffn, norm_final, w_in_ab, w_out_ab, sb_bias, gmlp_norm, gmlp_ws, gmlp_bs, w_in_c, w_out_c, hgrn_lb, hgrn_gnorm, xattn_wq, xattn_wkv, xattn_wo, ffn_w13, ffn_w2):
    bp, tp, d = x_prompt.shape
    bd, ts, _ = x_sample.shape
    assert ts == 1, "the sample group carries one new token per sequence"
    depth = norm_mix.shape[0]
    n_even = w_in_ab.shape[0]
    a_heads = sb_bias.shape[1]
    a_width = a_heads * LANES
    groups = gmlp_ws.shape[1]
    b_width = groups * LANES
    c_heads = hgrn_lb.shape[1] // LANES
    mem_len = mem_prompt.shape[1]
    xw = xattn_wq.shape[2]
    assert a_width == b_width and w_in_ab.shape[2] == 3 * a_width + 2 * b_width

    xp = x_prompt.reshape(bp * tp, d)
    xs = x_sample.reshape(bd, d)
    mem = mem_prompt.reshape(bp * mem_len, d)

    lb_all = jnp.cumsum(jax.nn.softmax(hgrn_lb.astype(F32), axis=0), axis=0)
    lb_all = lb_all - lb_all[:1]

    tm_p, tm_s = 1024, bd
    tn_cast = 512
    ffn_tn = 256

    w_kv_ab = w_in_ab[:, :, a_width:3 * a_width].astype(BF16)
    w2_all = ffn_w2.astype(BF16)
    wq_all = xattn_wq.astype(BF16)
    wo_all = xattn_wo.astype(BF16)
    wkv_all = xattn_wkv.astype(BF16)

    sbk_p, sbv_p = None, None
    sbk_s, sbv_s, gv_s, hs_p, hs_s, mk_p, mv_p = [], [], [], [], [], [], []
    for l in range(depth):
        j = l // 2
        hp = _rmsnorm(xp, norm_mix[l], BF16, "norm_mix_p")
        hs = _rmsnorm(xs, norm_mix[l], BF16, "norm_mix_s")
        if l % 2 == 0:
            qp, wq_b = _matmul([hp], w_in_ab, layer=j, w_cols=(0, a_width), cast_w=True, out_dtype=BF16,
                               tm=tm_p, tn=tn_cast, name="in_q_p")
            kp2, sbk_p = _matmul_heads(hp, w_kv_ab, layer=j, w_cols=(0, a_width), batch=bp, seq=tp, slot=j,
                                       n_slots=n_even, stacked=sbk_p, tm=tm_p, name="in_k_p")
            vp2, sbv_p = _matmul_heads(hp, w_kv_ab, layer=j, w_cols=(a_width, a_width), batch=bp, seq=tp, slot=j,
                                       n_slots=n_even, stacked=sbv_p, tm=tm_p, name="in_v_p")
            uv, wuv_b = _matmul([hp], w_in_ab, layer=j, w_cols=(3 * a_width, 2 * b_width), cast_w=True,
                                out_dtype=F32, tm=tm_p, tn=tn_cast, name="in_uv_p")
            oa = _sb_prompt(qp, kp2, vp2, sb_bias[j], batch=bp, seq=tp, heads=a_heads, tq=512, hp=2, name="sb_p")
            ob = _gmlp_prompt(uv, gmlp_norm[j], gmlp_ws[j], gmlp_bs[j], u_col=0, name="gmlp_p")
            xp, wout_b = _matmul([oa, ob], w_out_ab, layer=j, cast_w=True, out_dtype=F32, tm=tm_p, tn=tn_cast,
                                 resid=xp, name="out_ab_p")
            qs = _matmul([hs], wq_b, out_dtype=F32, tm=tm_s, tn=1024, name="in_q_s")
            kvs = _matmul([hs], w_kv_ab, layer=j, out_dtype=F32, tm=tm_s, tn=1024, name="in_kv_s")
            uvs = _matmul([hs], wuv_b, out_dtype=F32, tm=tm_s, tn=1024, name="in_uv_s")
            oa = _sb_sample(qs.reshape(bd, a_heads, LANES), sb_bias[j], cache_sb_k, cache_sb_v, page_table, j,
                            pps=4, name="sb_s")
            ob, vn = _gmlp_sample(uvs, gmlp_norm[j], gmlp_ws[j], gmlp_bs[j], u_col=0, name="gmlp_s")
            xs = _matmul([oa, ob], wout_b, out_dtype=F32, tm=tm_s, tn=1024, resid=xs, name="out_ab_s")
            sbk_s.append(kvs[:, :a_width].reshape(bd, ts, a_heads, LANES))
            sbv_s.append(kvs[:, a_width:].reshape(bd, ts, a_heads, LANES))
            gv_s.append(vn.reshape(bd, ts, groups, LANES))
        else:
            pp, win_b = _matmul([hp], w_in_c, layer=j, cast_w=True, out_dtype=F32, tm=tm_p, tn=tn_cast,
                                name="in_c_p")
            oc, s_p = _hgrn_prompt(pp, lb_all[j], hgrn_gnorm[j], batch=bp, seq=tp, heads=c_heads, tb=1024,
                                   name="hgrn_p")
            xp, wout_b = _matmul([oc], w_out_c, layer=j, cast_w=True, out_dtype=F32, tm=tm_p, tn=tn_cast,
                                 resid=xp, name="out_c_p")
            hs_p.append(s_p)
            ps = _matmul([hs], win_b, out_dtype=F32, tm=tm_s, tn=1024, name="in_c_s")
            oc, s_s = _hgrn_sample(ps, lb_all[j], hgrn_gnorm[j], state_hgrn, j, heads=c_heads, name="hgrn_s")
            xs = _matmul([oc], wout_b, out_dtype=F32, tm=tm_s, tn=1024, resid=xs, name="out_c_s")
            hs_s.append(s_s)
        hm = _rmsnorm(mem, norm_mem[l], BF16, "norm_mem")
        kv = _matmul([hm], wkv_all, layer=l, out_dtype=F32, tm=bp * mem_len, tn=512, name="mem_kv")
        kp = kv[:, :xw].reshape(bp, mem_len, xw)
        vp = kv[:, xw:].reshape(bp, mem_len, xw)
        mk_p.append(kp.reshape(bp, mem_len, xw // LANES, LANES))
        mv_p.append(vp.reshape(bp, mem_len, xw // LANES, LANES))
        xp, hfp = _xattn_prompt(xp, norm_xattn[l], norm_ffn[l], wq_all, wo_all, l, kp, vp, batch=bp, seq=tp, tq=256,
                                name="xattn_p")
        xs, hfs = _xattn_sample(xs, norm_xattn[l], norm_ffn[l], wq_all, wo_all, cache_mem_k, cache_mem_v, l,
                                "xattn_s")
        mid, w13_b = _matmul_swiglu(hfp, ffn_w13, layer=l, cast_w=True, tm=tm_p, tn=ffn_tn, name="ffn13_p")
        xp = _matmul([mid], w2_all, layer=l, out_dtype=F32, tm=512, tn=1024, resid=xp, w_single=True, name="ffn2_p")
        mid = _matmul_swiglu(hfs, w13_b, tm=tm_s, tn=ffn_tn, name="ffn13_s")
        xs = _matmul([mid], w2_all, layer=l, out_dtype=F32, tm=tm_s, tn=512, resid=xs, name="ffn2_s")
    y_prompt = _rmsnorm(xp, norm_final, F32, "norm_final_p").reshape(bp, tp, d)
    y_sample = _rmsnorm(xs, norm_final, F32, "norm_final_s").reshape(bd, ts, d)
    return (y_prompt, y_sample, sbk_p, sbv_p,
            jnp.stack(sbk_s, axis=1), jnp.stack(sbv_s, axis=1),
            jnp.stack(gv_s, axis=1),
            jnp.stack(hs_p, axis=0), jnp.stack(hs_s, axis=0),
            jnp.stack(mk_p, axis=0), jnp.stack(mv_p, axis=0))
```

```python
import functools

import numpy as np

import jax
import jax.numpy as jnp
from jax import lax
from jax.experimental import pallas as pl
from jax.experimental.pallas import tpu as pltpu

F32 = jnp.float32
BF16 = jnp.bfloat16
EPS = 1e-6
LANES = 128
SUBLANES = 8
VMEM_CAP = 60 * 1024 * 1024
VMEM_SLACK = 12 * 1024 * 1024
NT_DIMS = (((1,), (1,)), ((), ()))
TN_DIMS = (((0,), (0,)), ((), ()))


def _nbytes(shape, dtype):
    n = 1
    for s in shape:
        n *= s
    return n * jnp.dtype(dtype).itemsize


def _params(semantics, blocks):
    need = sum(_nbytes(b[0], b[1]) * (b[2] if len(b) > 2 else 2) for b in blocks) + VMEM_SLACK
    return pltpu.CompilerParams(dimension_semantics=semantics,
                                vmem_limit_bytes=int(min(need, VMEM_CAP)))


def _rms_rows(x, g):
    r = lax.rsqrt(jnp.mean(x * x, axis=-1, keepdims=True) + EPS)
    return (x * r) * g


def _bf16_pieces(x, terms):
    out = []
    for _ in range(terms):
        piece = x.astype(BF16)
        out.append(piece)
        x = x - piece.astype(F32)
    return out


def _neg_softplus(z):
    return -(jnp.maximum(z, 0.0) + jnp.log(1.0 + jnp.exp(-jnp.abs(z))))


def _sum_over_later(log_keep, later, terms):
    acc = None
    for piece in _bf16_pieces(log_keep, terms):
        part = jnp.dot(piece, later, preferred_element_type=F32)
        acc = part if acc is None else acc + part
    return acc


def _norm_body(x_ref, g_ref, o_ref):
    o_ref[...] = _rms_rows(x_ref[...], g_ref[...]).astype(o_ref.dtype)


def _rmsnorm(x, g, out_dtype, name):
    m, d = x.shape
    tm = min(m, 256)
    return pl.pallas_call(
        _norm_body,
        grid=(m // tm,),
        in_specs=[pl.BlockSpec((tm, d), lambda i: (i, 0)),
                  pl.BlockSpec((1, d), lambda i: (0, 0))],
        out_specs=pl.BlockSpec((tm, d), lambda i: (i, 0)),
        out_shape=jax.ShapeDtypeStruct((m, d), out_dtype),
        compiler_params=_params(("parallel",), [((tm, d), F32), ((tm, d), out_dtype)]),
        name=name,
    )(x, g.reshape(1, d))


def _w_spec(w, layer, k, tn, col_of, **mode):
    if w.ndim == 3:
        return pl.BlockSpec((None, k, tn), lambda j, i: (layer, 0, col_of(j)), **mode)
    return pl.BlockSpec((k, tn), lambda j, i: (0, col_of(j)), **mode)


BF16_ROWS = 16


def _rounding_specs(jobs, n_steps, step_of):
    in_specs, out_specs, out_shape, blocks = [], [], [], []
    for w, layer in jobs:
        rows, cols = w.shape[-2:]
        r = BF16_ROWS
        while rows % r or rows // r > n_steps:
            r += BF16_ROWS
        last = rows // r - 1
        in_specs.append(pl.BlockSpec((None, r, cols),
                                     lambda *g, layer=layer, last=last: (layer, jnp.minimum(step_of(*g), last), 0)))
        out_specs.append(pl.BlockSpec((r, cols), lambda *g, last=last: (jnp.minimum(step_of(*g), last), 0)))
        out_shape.append(jax.ShapeDtypeStruct((rows, cols), BF16))
        blocks += [((r, cols), F32), ((r, cols), BF16)]
    return in_specs, out_specs, out_shape, blocks


def _round_blocks(src_refs, dst_refs):
    for src, dst in zip(src_refs, dst_refs):
        dst[...] = src[...].astype(BF16)


def _mm_body(*refs, k_sizes, has_resid):
    n_a = len(k_sizes)
    a_refs, w_ref, o_ref = refs[:n_a], refs[n_a], refs[-1]
    acc, off = None, 0
    for a_ref, kk in zip(a_refs, k_sizes):
        part = jnp.dot(a_ref[...], w_ref[off:off + kk, :], preferred_element_type=F32)
        acc = part if acc is None else acc + part
        off += kk
    if has_resid:
        acc = acc + refs[n_a + 1][...]
    o_ref[...] = acc.astype(o_ref.dtype)


def _matmul(a_list, w, *, out_dtype, tm, tn, layer=None, resid=None, w_cols=None, w_single=False, name):
    m = a_list[0].shape[0]
    k_sizes = tuple(a.shape[1] for a in a_list)
    k = w.shape[-2]
    first, n = w_cols if w_cols is not None else (0, w.shape[-1])
    assert sum(k_sizes) == k and m % tm == 0 and n % tn == 0 and first % tn == 0
    j0 = first // tn
    in_specs = [pl.BlockSpec((tm, kk), lambda j, i: (i, 0)) for kk in k_sizes]
    w_mode = dict(pipeline_mode=pl.Buffered(1)) if w_single else {}
    in_specs.append(_w_spec(w, layer, k, tn, lambda j: j0 + j, **w_mode))
    blocks = [((tm, kk), BF16) for kk in k_sizes] + [((k, tn), w.dtype, 1 if w_single else 2), ((tm, tn), out_dtype)]
    args = list(a_list) + [w]
    if resid is not None:
        in_specs.append(pl.BlockSpec((tm, tn), lambda j, i: (i, j)))
        blocks.append(((tm, tn), F32))
        args.append(resid)
    return pl.pallas_call(
        functools.partial(_mm_body, k_sizes=k_sizes, has_resid=resid is not None),
        grid=(n // tn, m // tm),
        in_specs=in_specs,
        out_specs=pl.BlockSpec((tm, tn), lambda j, i: (i, j)),
        out_shape=jax.ShapeDtypeStruct((m, n), out_dtype),
        compiler_params=_params(("parallel", "parallel"), blocks),
        name=name,
    )(*args)


def _mm_heads_body(a_ref, w_ref, *rest, hb):
    o2_ref, o5_ref = rest[-2], rest[-1]
    acc = jnp.dot(a_ref[...], w_ref[...], preferred_element_type=F32)
    o2_ref[...] = acc.astype(o2_ref.dtype)
    for h in range(hb):
        o5_ref[:, h, :] = acc[:, h * LANES:(h + 1) * LANES]


def _matmul_heads(a, w, *, layer, w_cols, batch, seq, slot, n_slots, stacked, tm, name):
    m, k = a.shape
    first, n = w_cols
    heads = n // LANES
    hb = SUBLANES
    tn = hb * LANES
    tps = seq // tm
    assert m == batch * seq and seq % tm == 0 and n % tn == 0 and first % tn == 0
    j0 = first // tn
    in_specs = [pl.BlockSpec((tm, k), lambda j, i: (i, 0)),
                _w_spec(w, layer, k, tn, lambda j: j0 + j)]
    args = [a, w]
    aliases = {}
    if stacked is not None:
        in_specs.append(pl.BlockSpec(memory_space=pl.ANY))
        args.append(stacked)
        aliases = {2: 1}
    return pl.pallas_call(
        functools.partial(_mm_heads_body, hb=hb),
        grid=(n // tn, m // tm),
        in_specs=in_specs,
        out_specs=[pl.BlockSpec((tm, tn), lambda j, i: (i, j)),
                   pl.BlockSpec((None, None, tm, hb, LANES), lambda j, i: (i // tps, slot, i % tps, j, 0))],
        out_shape=[jax.ShapeDtypeStruct((m, n), BF16),
                   jax.ShapeDtypeStruct((batch, n_slots, seq, heads, LANES), F32)],
        input_output_aliases=aliases,
        compiler_params=_params(("parallel", "parallel"),
                                [((tm, k), BF16), ((k, tn), BF16), ((tm, tn), BF16), ((tm, tn), F32)]),
        name=name,
    )(*args)


def _swiglu_body(a_ref, wa_ref, wb_ref, *refs):
    n_round = (len(refs) - 1) // 2
    o_ref = refs[n_round]
    a = a_ref[...]
    ga = jnp.dot(a, wa_ref[...], preferred_element_type=F32)
    gb = jnp.dot(a, wb_ref[...], preferred_element_type=F32)
    o_ref[...] = (jax.nn.silu(ga) * gb).astype(o_ref.dtype)
    _round_blocks(refs[:n_round], refs[n_round + 1:])


def _matmul_swiglu(a, w13, *, tm, tn, rounding=(), name):
    m, k = a.shape
    hidden = w13.shape[1] // 2
    assert m % tm == 0 and hidden % tn == 0
    nj = hidden // tn
    r_in, r_out, r_shape, r_blocks = _rounding_specs(rounding, (m // tm) * nj, lambda i, j: i * nj + j)
    res = pl.pallas_call(
        _swiglu_body,
        grid=(m // tm, nj),
        in_specs=[pl.BlockSpec((tm, k), lambda i, j: (i, 0)),
                  pl.BlockSpec((k, tn), lambda i, j: (0, j)),
                  pl.BlockSpec((k, tn), lambda i, j: (0, j + nj))] + r_in,
        out_specs=[pl.BlockSpec((tm, tn), lambda i, j: (i, j))] + r_out,
        out_shape=[jax.ShapeDtypeStruct((m, hidden), BF16)] + r_shape,
        compiler_params=_params(("arbitrary", "arbitrary"),
                                [((tm, k), BF16), ((k, tn), BF16), ((k, tn), BF16), ((tm, tn), BF16)] + r_blocks),
        name=name,
    )(a, w13, w13, *[w for w, _ in rounding])
    return res[0] if not rounding else tuple(res)


def _sb_prompt_body(bias_ref, q_ref, k_ref, v_ref, *refs, tq, hp, scale):
    n_round = (len(refs) - 1) // 2
    o_ref = refs[n_round]
    _round_blocks(refs[:n_round], refs[n_round + 1:])
    qi = pl.program_id(2)
    d = LANES
    row = lax.broadcasted_iota(jnp.int32, (tq, tq), 0)
    col = lax.broadcasted_iota(jnp.int32, (tq, tq), 1)
    later = (row > col).astype(BF16)
    causal = col < row

    def block(kb, h, acc, carry, mask):
        hs = slice(h * d, (h + 1) * d)
        ks = k_ref[pl.ds(kb * tq, tq), hs]
        vs = v_ref[pl.ds(kb * tq, tq), hs]
        z = lax.dot_general(q_ref[:, hs], ks, NT_DIMS, preferred_element_type=F32)
        z = z * scale + bias_ref[h]
        log_keep = _neg_softplus(z)
        if mask:
            log_keep = jnp.where(causal, log_keep, 0.0)
        after = _sum_over_later(log_keep, later, 2)
        w = jnp.exp(log_keep + z + after + carry)
        if mask:
            w = jnp.where(causal, w, 0.0)
        acc = acc + jnp.dot(w.astype(BF16), vs, preferred_element_type=F32)
        carry = carry + after[:, :1] + log_keep[:, :1]
        return acc, carry

    state = []
    for h in range(hp):
        state.extend(block(qi, h, jnp.zeros((tq, d), F32), jnp.zeros((tq, 1), F32), True))

    def body(it, c):
        out = []
        for h in range(hp):
            out.extend(block(qi - 1 - it, h, c[2 * h], c[2 * h + 1], False))
        return tuple(out)

    state = lax.fori_loop(0, qi, body, tuple(state))
    for h in range(hp):
        o_ref[:, h * d:(h + 1) * d] = state[2 * h].astype(o_ref.dtype)


def _sb_prompt(q, k, v, bias, *, batch, seq, heads, tq, hp, rounding=(), name):
    d = LANES
    nq = seq // tq
    ng = heads // hp
    bias_rows = jnp.broadcast_to(bias.astype(F32)[:, None, None], (heads, 1, tq))
    q_spec = pl.BlockSpec((tq, hp * d), lambda b, h, i: (b * nq + i, h))
    kv_spec = pl.BlockSpec((seq, hp * d), lambda b, h, i: (b, h))
    r_in, r_out, r_shape, r_blocks = _rounding_specs(rounding, batch * ng * nq,
                                                     lambda b, h, i: (b * ng + h) * nq + i)
    res = pl.pallas_call(
        functools.partial(_sb_prompt_body, tq=tq, hp=hp, scale=d ** -0.5),
        grid=(batch, ng, nq),
        in_specs=[pl.BlockSpec((hp, 1, tq), lambda b, h, i: (h, 0, 0)), q_spec, kv_spec, kv_spec] + r_in,
        out_specs=[q_spec] + r_out,
        out_shape=[jax.ShapeDtypeStruct((batch * seq, heads * d), BF16)] + r_shape,
        compiler_params=_params(("arbitrary", "arbitrary", "arbitrary"),
                                [((tq, hp * d), BF16)] * 2 + [((seq, hp * d), BF16)] * 2 + r_blocks),
        name=name,
    )(bias_rows, q, k, v, *[w for w, _ in rounding])
    return res[0] if not rounding else tuple(res)


def _sb_sample_body(pt_ref, q_ref, bias_ref, *refs, heads, pps, scale):
    del pt_ref
    k_refs, v_refs = refs[:pps], refs[pps:2 * pps]
    o_ref, acc_ref, carry_ref = refs[2 * pps:]
    pg = pl.program_id(1)
    page = k_refs[0].shape[0]
    d = LANES

    @pl.when(pg == 0)
    def _():
        acc_ref[...] = jnp.zeros_like(acc_ref)
        carry_ref[...] = jnp.zeros_like(carry_ref)

    row = lax.broadcasted_iota(jnp.int32, (page, page), 0)
    col = lax.broadcasted_iota(jnp.int32, (page, page), 1)
    later = (row > col).astype(BF16)
    hrow = lax.broadcasted_iota(jnp.int32, (heads, page), 0)
    hrow_d = lax.broadcasted_iota(jnp.int32, (heads, d), 0)
    q = q_ref[...]
    bias = bias_ref[...]
    acc = acc_ref[...]
    carry = carry_ref[...]
    for k_ref, v_ref in zip(k_refs, v_refs):
        kt = jnp.swapaxes(k_ref[...], 0, 1).astype(BF16)
        vt = jnp.swapaxes(v_ref[...], 0, 1).astype(BF16)
        z = jnp.zeros((heads, page), F32)
        for h in range(heads):
            zh = lax.dot_general(q, kt[h], NT_DIMS, preferred_element_type=F32)
            z = jnp.where(hrow == h, zh, z)
        z = z * scale + bias
        log_keep = _neg_softplus(z)
        after = _sum_over_later(log_keep, later, 3)
        w = jnp.exp(log_keep + z + after + carry).astype(BF16)
        carry = carry + after[:, :1] + log_keep[:, :1]
        for h in range(heads):
            oh = jnp.dot(w, vt[h], preferred_element_type=F32)
            acc = acc + jnp.where(hrow_d == h, oh, 0.0)
    acc_ref[...] = acc
    carry_ref[...] = carry

    @pl.when(pg == pl.num_programs(1) - 1)
    def _():
        o_ref[...] = acc.astype(o_ref.dtype)


def _sb_sample(q, bias, cache_k, cache_v, page_table, layer, *, pps, name):
    bsz, heads, d = q.shape
    page = cache_k.shape[2]
    n_pages = page_table.shape[1]
    assert n_pages % pps == 0
    bias_rows = jnp.broadcast_to(bias.astype(F32)[:, None], (heads, page))

    def cache_spec(r):
        return pl.BlockSpec((None, None, page, heads, d),
                            lambda b, p, pt: (pt[b, n_pages - 1 - (p * pps + r)], layer, 0, 0, 0))

    grid_spec = pltpu.PrefetchScalarGridSpec(
        num_scalar_prefetch=1,
        grid=(bsz, n_pages // pps),
        in_specs=[pl.BlockSpec((None, heads, d), lambda b, p, pt: (b, 0, 0)),
                  pl.BlockSpec((heads, page), lambda b, p, pt: (0, 0))]
                 + [cache_spec(r) for r in range(pps)] * 2,
        out_specs=pl.BlockSpec((None, heads, d), lambda b, p, pt: (b, 0, 0)),
        scratch_shapes=[pltpu.VMEM((heads, d), F32), pltpu.VMEM((heads, 1), F32)],
    )
    out = pl.pallas_call(
        functools.partial(_sb_sample_body, heads=heads, pps=pps, scale=d ** -0.5),
        grid_spec=grid_spec,
        out_shape=jax.ShapeDtypeStruct((bsz, heads, d), BF16),
        compiler_params=_params(("parallel", "arbitrary"), [((page, heads, d), F32)] * (2 * pps)),
        name=name,
    )(page_table, q.astype(BF16), bias_rows, *([cache_k] * pps), *([cache_v] * pps))
    return out.reshape(bsz, heads * d)


def _gmlp_prompt_body(u_ref, vb_ref, gn_ref, ws_ref, bst_ref, o_ref, *, groups):
    chunk = u_ref.shape[0]
    vn = _rms_rows(jax.nn.gelu(vb_ref[...]), gn_ref[...])
    row = lax.broadcasted_iota(jnp.int32, (chunk, chunk), 0)
    col = lax.broadcasted_iota(jnp.int32, (chunk, chunk), 1)
    for g in range(groups):
        sl = slice(g * LANES, (g + 1) * LANES)
        w = jnp.where(col <= row, ws_ref[g], 0.0).astype(BF16)
        s = jnp.dot(w, vn[:, sl].astype(BF16), preferred_element_type=F32) + bst_ref[:, g:g + 1]
        o_ref[:, sl] = (jax.nn.gelu(u_ref[:, sl]) * s).astype(o_ref.dtype)


def _gmlp_prompt(p, gnorm, ws, bs, *, u_col, name):
    groups, chunk, _ = ws.shape
    width = groups * LANES
    m = p.shape[0]
    return pl.pallas_call(
        functools.partial(_gmlp_prompt_body, groups=groups),
        grid=(m // chunk,),
        in_specs=[pl.BlockSpec((chunk, width), lambda i: (i, u_col)),
                  pl.BlockSpec((chunk, width), lambda i: (i, u_col + 1)),
                  pl.BlockSpec((1, width), lambda i: (0, 0)),
                  pl.BlockSpec((groups, chunk, chunk), lambda i: (0, 0, 0)),
                  pl.BlockSpec((chunk, groups), lambda i: (0, 0))],
        out_specs=pl.BlockSpec((chunk, width), lambda i: (i, 0)),
        out_shape=jax.ShapeDtypeStruct((m, width), BF16),
        compiler_params=_params(("parallel",),
                                [((chunk, width), F32)] * 2 + [((groups, chunk, chunk), F32), ((chunk, width), BF16)]),
        name=name,
    )(p, p, gnorm.reshape(1, width), ws, bs.T)


def _gmlp_sample_body(u_ref, vb_ref, gn_ref, w0_ref, b0_ref, o_ref, vn_ref):
    vn = _rms_rows(jax.nn.gelu(vb_ref[...]), gn_ref[...])
    vn_ref[...] = vn
    o_ref[...] = (jax.nn.gelu(u_ref[...]) * (w0_ref[...] * vn + b0_ref[...])).astype(o_ref.dtype)


def _gmlp_sample(p, gnorm, ws, bs, *, u_col, name):
    groups = ws.shape[0]
    width = groups * LANES
    m = p.shape[0]
    w0 = jnp.repeat(ws[:, 0, 0], LANES).reshape(1, width)
    b0 = jnp.repeat(bs[:, 0], LANES).reshape(1, width)
    row = lambda c: pl.BlockSpec((m, width), lambda i: (0, c))
    vec = pl.BlockSpec((1, width), lambda i: (0, 0))
    return pl.pallas_call(
        _gmlp_sample_body,
        grid=(1,),
        in_specs=[row(u_col), row(u_col + 1), vec, vec, vec],
        out_specs=[row(0), row(0)],
        out_shape=[jax.ShapeDtypeStruct((m, width), BF16), jax.ShapeDtypeStruct((m, width), F32)],
        compiler_params=_params(("arbitrary",), [((m, width), F32)] * 4),
        name=name,
    )(p, p, gnorm.reshape(1, width), w0, b0)


HGRN_LEVELS = 7


def _hgrn_gates(fpre, lb):
    t = jnp.log(1.0 + jnp.exp(-jnp.abs(fpre)))
    log_sig = jnp.minimum(fpre, 0.0) - t
    log_sig_neg = jnp.minimum(-fpre, 0.0) - t
    a = jnp.log(lb)
    c = jnp.log1p(-lb) + log_sig
    logf = jnp.maximum(a, c) + jnp.log(1.0 + jnp.exp(-jnp.abs(a - c)))
    return logf, (1.0 - lb) * jnp.exp(log_sig_neg)


def _hgrn_finish(o, g, gn):
    return _rms_rows(o, gn) * jax.nn.silu(g)


def _hgrn_tables():
    n = LANES
    t = np.arange(n)[:, None]
    r = np.arange(n)[None, :]
    blocks = []
    for level in range(HGRN_LEVELS):
        c = 1 << level
        mid = (t // (2 * c)) * (2 * c) + c
        second = (t % (2 * c)) >= c
        blocks.append(np.where(second, (r >= mid) & (r <= t), (r > t) & (r < mid)))
    blocks += [r <= t, r > t, np.ones((SUBLANES, n), bool)]
    ranges = np.concatenate(blocks, axis=0).astype(np.float32)
    diff = t ^ r
    level_of = np.where(t == r, -1, np.where(r < t, np.floor(np.log2(np.maximum(diff, 1))), HGRN_LEVELS))
    return (jnp.asarray(np.concatenate([ranges, ranges], axis=1), BF16),
            jnp.asarray(level_of.astype(np.int32)))


def _hgrn_prompt_body(q_ref, f_ref, i_ref, g_ref, lb_ref, gn_ref, rng_ref, lvl_ref, *refs, tb):
    n_round = (len(refs) - 3) // 2
    o_ref, s_ref = refs[n_round:n_round + 2]
    st_ref = refs[-1]
    _round_blocks(refs[:n_round], refs[n_round + 2:-1])
    step = pl.program_id(2)
    n = LANES

    @pl.when(step == 0)
    def _():
        st_ref[...] = jnp.zeros_like(st_ref)

    lb = lb_ref[...]
    gn = gn_ref[...]
    level_of = lvl_ref[...]
    st = st_ref[...]
    tiles = [slice(i * n, (i + 1) * n) for i in range(tb // n)]
    logf_all, k_all = _hgrn_gates(f_ref[...], lb)
    rhs = jnp.concatenate([jnp.concatenate([piece[rs] for rs in tiles], axis=1)
                           for piece in _bf16_pieces(logf_all, 2)], axis=0)
    decay_all = jnp.exp(jnp.dot(rng_ref[...], rhs, preferred_element_type=F32))
    for rs in tiles:
        q = q_ref[rs, :]
        v = i_ref[rs, :]
        k = k_all[rs]
        decay = decay_all[:, rs]
        vb = v.astype(BF16)
        scores = jnp.zeros((n, n), F32)
        for level in range(HGRN_LEVELS):
            dl = decay[level * n:(level + 1) * n]
            pair = lax.dot_general((q * dl).astype(BF16), (k * dl).astype(BF16), NT_DIMS,
                                   preferred_element_type=F32)
            scores = jnp.where(level_of == level, pair, scores)
        same = lax.dot_general(q.astype(BF16), k.astype(BF16), NT_DIMS, preferred_element_type=F32)
        scores = jnp.where(level_of == -1, same, scores)
        from_start = decay[HGRN_LEVELS * n:(HGRN_LEVELS + 1) * n]
        to_end = decay[(HGRN_LEVELS + 1) * n:(HGRN_LEVELS + 2) * n]
        total = decay[(HGRN_LEVELS + 2) * n:(HGRN_LEVELS + 2) * n + 1]
        o = (jnp.dot(scores.astype(BF16), vb, preferred_element_type=F32)
             + lax.dot_general((q * from_start).astype(BF16), st.astype(BF16), NT_DIMS,
                               preferred_element_type=F32))
        st = st * total + lax.dot_general(vb, (k * to_end).astype(BF16), TN_DIMS, preferred_element_type=F32)
        o_ref[rs, :] = _hgrn_finish(o, g_ref[rs, :], gn).astype(o_ref.dtype)
    st_ref[...] = st

    @pl.when(step == pl.num_programs(2) - 1)
    def _():
        s_ref[...] = st.T


def _hgrn_prompt(p, lb, gnorm, *, batch, seq, heads, tb, rounding=(), name):
    d = LANES
    nt = seq // tb
    ranges, level_of = _hgrn_tables()
    col = lambda c: pl.BlockSpec((tb, d), lambda b, h, t: (b * nt + t, c * heads + h))
    r_in, r_out, r_shape, r_blocks = _rounding_specs(rounding, batch * heads * nt,
                                                     lambda b, h, t: (b * heads + h) * nt + t)
    res = pl.pallas_call(
        functools.partial(_hgrn_prompt_body, tb=tb),
        grid=(batch, heads, nt),
        in_specs=[col(0), col(1), col(2), col(3),
                  pl.BlockSpec((None, 1, d), lambda b, h, t: (h, 0, 0)),
                  pl.BlockSpec((1, d), lambda b, h, t: (0, 0)),
                  pl.BlockSpec(ranges.shape, lambda b, h, t: (0, 0)),
                  pl.BlockSpec(level_of.shape, lambda b, h, t: (0, 0))] + r_in,
        out_specs=[pl.BlockSpec((tb, d), lambda b, h, t: (b * nt + t, h)),
                   pl.BlockSpec((None, None, d, d), lambda b, h, t: (b, h, 0, 0))] + r_out,
        out_shape=[jax.ShapeDtypeStruct((batch * seq, heads * d), BF16),
                   jax.ShapeDtypeStruct((batch, heads, d, d), F32)] + r_shape,
        scratch_shapes=[pltpu.VMEM((d, d), F32)],
        compiler_params=_params(("arbitrary", "arbitrary", "arbitrary"),
                                [((tb, d), F32)] * 5 + [((d, d), F32), (ranges.shape, BF16)] + r_blocks),
        name=name,
    )(p, p, p, p, lb.reshape(heads, 1, d), gnorm.reshape(1, d), ranges, level_of, *[w for w, _ in rounding])
    return tuple(res)


def _hgrn_sample_body(q_ref, f_ref, i_ref, g_ref, lb_ref, gn_ref, s_ref, o_ref, so_ref, oacc_ref, *, heads):
    d = LANES
    q = q_ref[...]
    v = i_ref[...]
    logf, k = _hgrn_gates(f_ref[...], lb_ref[...])
    f = jnp.exp(logf)
    eye = lax.broadcasted_iota(jnp.int32, (d, d), 0) == lax.broadcasted_iota(jnp.int32, (d, d), 1)

    def to_col(r):
        return jnp.sum(jnp.where(eye, r, 0.0), axis=1, keepdims=True)

    for h in range(heads):
        hs = slice(h, h + 1)
        s_new = to_col(f[hs]) * s_ref[h] + to_col(k[hs]) * v[hs]
        so_ref[h] = s_new
        oacc_ref[hs, :] = jnp.sum(to_col(q[hs]) * s_new, axis=0, keepdims=True)
    o_ref[...] = _hgrn_finish(oacc_ref[...], g_ref[...], gn_ref[...]).astype(o_ref.dtype)


def _hgrn_sample(p, lb, gnorm, state, layer, *, heads, name):
    bsz = p.shape[0]
    d = LANES
    p4 = p.reshape(bsz, 4, heads, d)
    part = lambda c: pl.BlockSpec((None, None, heads, d), lambda b: (b, c, 0, 0))
    o, s_new = pl.pallas_call(
        functools.partial(_hgrn_sample_body, heads=heads),
        grid=(bsz,),
        in_specs=[part(0), part(1), part(2), part(3),
                  pl.BlockSpec((heads, d), lambda b: (0, 0)),
                  pl.BlockSpec((1, d), lambda b: (0, 0)),
                  pl.BlockSpec((None, None, heads, d, d), lambda b: (layer, b, 0, 0, 0))],
        out_specs=[pl.BlockSpec((None, heads, d), lambda b: (b, 0, 0)),
                   pl.BlockSpec((None, heads, d, d), lambda b: (b, 0, 0, 0))],
        out_shape=[jax.ShapeDtypeStruct((bsz, heads, d), BF16),
                   jax.ShapeDtypeStruct((bsz, heads, d, d), F32)],
        scratch_shapes=[pltpu.VMEM((heads, d), F32)],
        compiler_params=_params(("parallel",), [((heads, d, d), F32)] * 2),
        name=name,
    )(p4, p4, p4, p4, lb.reshape(heads, d), gnorm.reshape(1, d), state)
    return o.reshape(bsz, heads * d), s_new


def _attend(q, k_ref, v_ref, heads, scale):
    outs = []
    for h in range(heads):
        sl = slice(h * LANES, (h + 1) * LANES)
        s = lax.dot_general(q[:, sl].astype(BF16), k_ref[:, sl].astype(BF16), NT_DIMS,
                            preferred_element_type=F32) * scale
        e = jnp.exp(s - jnp.max(s, axis=-1, keepdims=True))
        prob = e * (1.0 / jnp.sum(e, axis=-1, keepdims=True))
        outs.append(jnp.dot(prob.astype(BF16), v_ref[:, sl].astype(BF16), preferred_element_type=F32))
    return jnp.concatenate(outs, axis=1)


def _xattn_prompt_body(x_ref, gx_ref, gf_ref, wq_ref, wo_ref, k_ref, v_ref, y_ref, hf_ref, *, heads, scale):
    x = x_ref[...]
    h = _rms_rows(x, gx_ref[...]).astype(BF16)
    q = jnp.dot(h, wq_ref[...], preferred_element_type=F32)
    att = _attend(q, k_ref, v_ref, heads, scale).astype(BF16)
    y = x + jnp.dot(att, wo_ref[...], preferred_element_type=F32)
    y_ref[...] = y
    hf_ref[...] = _rms_rows(y, gf_ref[...]).astype(hf_ref.dtype)


def _xattn_prompt(x, gx, gf, wq, wo, layer, mem_k, mem_v, *, batch, seq, tq, name):
    d = x.shape[1]
    xw = wq.shape[2]
    heads = xw // LANES
    mlen = mem_k.shape[1]
    nt = seq // tq
    rows = pl.BlockSpec((tq, d), lambda b, t: (b * nt + t, 0))
    vec = pl.BlockSpec((1, d), lambda b, t: (0, 0))
    mem = pl.BlockSpec((None, mlen, xw), lambda b, t: (b, 0, 0))
    return pl.pallas_call(
        functools.partial(_xattn_prompt_body, heads=heads, scale=LANES ** -0.5),
        grid=(batch, nt),
        in_specs=[rows, vec, vec,
                  pl.BlockSpec((None, d, xw), lambda b, t: (layer, 0, 0)),
                  pl.BlockSpec((None, xw, d), lambda b, t: (layer, 0, 0)),
                  mem, mem],
        out_specs=[rows, rows],
        out_shape=[jax.ShapeDtypeStruct(x.shape, F32), jax.ShapeDtypeStruct(x.shape, BF16)],
        compiler_params=_params(("parallel", "parallel"),
                                [((tq, d), F32)] * 2 + [((tq, d), BF16), ((d, xw), BF16), ((xw, d), BF16),
                                                        ((mlen, xw), F32), ((mlen, xw), F32)]),
        name=name,
    )(x, gx.reshape(1, d), gf.reshape(1, d), wq, wo, mem_k, mem_v)


def _xattn_sample_body(x_ref, gx_ref, gf_ref, wq_ref, wo_ref, k_ref, v_ref, y_ref, hf_ref, q_sc, att_sc,
                       *, heads, scale):
    b = pl.program_id(0)

    @pl.when(b == 0)
    def _():
        h = _rms_rows(x_ref[...], gx_ref[...]).astype(BF16)
        q_sc[...] = jnp.dot(h, wq_ref[...], preferred_element_type=F32)
        att_sc[...] = jnp.zeros_like(att_sc)

    att = _attend(q_sc[...], k_ref, v_ref, heads, scale)
    mine = lax.broadcasted_iota(jnp.int32, att.shape, 0) == b
    att_sc[...] += jnp.where(mine, att, 0.0)

    @pl.when(b == pl.num_programs(0) - 1)
    def _():
        y = x_ref[...] + jnp.dot(att_sc[...].astype(BF16), wo_ref[...], preferred_element_type=F32)
        y_ref[...] = y
        hf_ref[...] = _rms_rows(y, gf_ref[...]).astype(hf_ref.dtype)


def _xattn_sample(x, gx, gf, wq, wo, mem_k, mem_v, layer, name):
    bsz, d = x.shape
    xw = wq.shape[2]
    heads = xw // LANES
    mlen = mem_k.shape[2]
    mk = mem_k.reshape(mem_k.shape[0], bsz, mlen, xw)
    mv = mem_v.reshape(mem_v.shape[0], bsz, mlen, xw)
    rows = pl.BlockSpec((bsz, d), lambda b: (0, 0))
    vec = pl.BlockSpec((1, d), lambda b: (0, 0))
    mem = pl.BlockSpec((None, None, mlen, xw), lambda b: (layer, b, 0, 0))
    return pl.pallas_call(
        functools.partial(_xattn_sample_body, heads=heads, scale=LANES ** -0.5),
        grid=(bsz,),
        in_specs=[rows, vec, vec,
                  pl.BlockSpec((None, d, xw), lambda b: (layer, 0, 0)),
                  pl.BlockSpec((None, xw, d), lambda b: (layer, 0, 0)),
                  mem, mem],
        out_specs=[rows, rows],
        out_shape=[jax.ShapeDtypeStruct(x.shape, F32), jax.ShapeDtypeStruct(x.shape, BF16)],
        scratch_shapes=[pltpu.VMEM((bsz, xw), F32), pltpu.VMEM((bsz, xw), F32)],
        compiler_params=_params(("arbitrary",),
                                [((d, xw), BF16), ((xw, d), BF16), ((mlen, xw), F32), ((mlen, xw), F32)]),
        name=name,
    )(x, gx.reshape(1, d), gf.reshape(1, d), wq, wo, mk, mv)


def kernel(x_prompt, x_sample, mem_prompt, cache_sb_k, cache_sb_v, cache_mem_k, cache_mem_v, state_hgrn, page_table, norm_mix, norm_mem, norm_xattn, norm_ffn, norm_final, w_in_ab, w_out_ab, sb_bias, gmlp_norm, gmlp_ws, gmlp_bs, w_in_c, w_out_c, hgrn_lb, hgrn_gnorm, xattn_wq, xattn_wkv, xattn_wo, ffn_w13, ffn_w2):
    bp, tp, d = x_prompt.shape
    bd, ts, _ = x_sample.shape
    assert ts == 1, "the sample group carries one new token per sequence"
    depth = norm_mix.shape[0]
    n_even = w_in_ab.shape[0]
    a_heads = sb_bias.shape[1]
    a_width = a_heads * LANES
    groups = gmlp_ws.shape[1]
    b_width = groups * LANES
    c_heads = hgrn_lb.shape[1] // LANES
    mem_len = mem_prompt.shape[1]
    xw = xattn_wq.shape[2]
    assert a_width == b_width and w_in_ab.shape[2] == 3 * a_width + 2 * b_width

    xp = x_prompt.reshape(bp * tp, d)
    xs = x_sample.reshape(bd, d)
    mem = mem_prompt.reshape(bp * mem_len, d)

    lb_all = jnp.cumsum(jax.nn.softmax(hgrn_lb.astype(F32), axis=0), axis=0)
    lb_all = lb_all - lb_all[:1]

    tm_p, tm_s = 1024, bd
    ffn_tn = 256

    w_in = w_in_ab[0].astype(BF16)
    wq_all = xattn_wq.astype(BF16)
    wo_all = xattn_wo.astype(BF16)
    wkv_all = xattn_wkv.astype(BF16)

    sbk_p, sbv_p = None, None
    sbk_s, sbv_s, gv_s, hs_p, hs_s, mk_p, mv_p = [], [], [], [], [], [], []
    for l in range(depth):
        j = l // 2
        hp = _rmsnorm(xp, norm_mix[l], BF16, "norm_mix_p")
        hs = _rmsnorm(xs, norm_mix[l], BF16, "norm_mix_s")
        if l % 2 == 0:
            qp = _matmul([hp], w_in, w_cols=(0, a_width), out_dtype=BF16, tm=tm_p, tn=1024, name="in_q_p")
            kp2, sbk_p = _matmul_heads(hp, w_in, layer=None, w_cols=(a_width, a_width), batch=bp, seq=tp, slot=j,
                                       n_slots=n_even, stacked=sbk_p, tm=tm_p, name="in_k_p")
            vp2, sbv_p = _matmul_heads(hp, w_in, layer=None, w_cols=(2 * a_width, a_width), batch=bp, seq=tp,
                                       slot=j, n_slots=n_even, stacked=sbv_p, tm=tm_p, name="in_v_p")
            uv = _matmul([hp], w_in, w_cols=(3 * a_width, 2 * b_width), out_dtype=F32, tm=tm_p, tn=1024,
                         name="in_uv_p")
            oa, w13, w_out = _sb_prompt(qp, kp2, vp2, sb_bias[j], batch=bp, seq=tp, heads=a_heads, tq=512, hp=2,
                                        rounding=[(ffn_w13, l), (w_out_ab, j)], name="sb_p")
            ob = _gmlp_prompt(uv, gmlp_norm[j], gmlp_ws[j], gmlp_bs[j], u_col=0, name="gmlp_p")
            xp = _matmul([oa, ob], w_out, out_dtype=F32, tm=tm_p, tn=1024, resid=xp, name="out_ab_p")
            ps = _matmul([hs], w_in, out_dtype=F32, tm=tm_s, tn=1024, name="in_ab_s")
            oa = _sb_sample(ps[:, :a_width].reshape(bd, a_heads, LANES), sb_bias[j], cache_sb_k, cache_sb_v,
                            page_table, j, pps=4, name="sb_s")
            ob, vn = _gmlp_sample(ps, gmlp_norm[j], gmlp_ws[j], gmlp_bs[j], u_col=3 * a_width // b_width,
                                  name="gmlp_s")
            xs = _matmul([oa, ob], w_out, out_dtype=F32, tm=tm_s, tn=1024, resid=xs, name="out_ab_s")
            sbk_s.append(ps[:, a_width:2 * a_width].reshape(bd, ts, a_heads, LANES))
            sbv_s.append(ps[:, 2 * a_width:3 * a_width].reshape(bd, ts, a_heads, LANES))
            gv_s.append(vn.reshape(bd, ts, groups, LANES))
        else:
            pp = _matmul([hp], w_in, out_dtype=F32, tm=tm_p, tn=1024, name="in_c_p")
            oc, s_p, w13, w_out = _hgrn_prompt(pp, lb_all[j], hgrn_gnorm[j], batch=bp, seq=tp, heads=c_heads,
                                               tb=1024, rounding=[(ffn_w13, l), (w_out_c, j)], name="hgrn_p")
            xp = _matmul([oc], w_out, out_dtype=F32, tm=tm_p, tn=1024, resid=xp, name="out_c_p")
            hs_p.append(s_p)
            ps = _matmul([hs], w_in, out_dtype=F32, tm=tm_s, tn=1024, name="in_c_s")
            oc, s_s = _hgrn_sample(ps, lb_all[j], hgrn_gnorm[j], state_hgrn, j, heads=c_heads, name="hgrn_s")
            xs = _matmul([oc], w_out, out_dtype=F32, tm=tm_s, tn=1024, resid=xs, name="out_c_s")
            hs_s.append(s_s)
        hm = _rmsnorm(mem, norm_mem[l], BF16, "norm_mem")
        kv = _matmul([hm], wkv_all, layer=l, out_dtype=F32, tm=bp * mem_len, tn=512, name="mem_kv")
        kp = kv[:, :xw].reshape(bp, mem_len, xw)
        vp = kv[:, xw:].reshape(bp, mem_len, xw)
        mk_p.append(kp.reshape(bp, mem_len, xw // LANES, LANES))
        mv_p.append(vp.reshape(bp, mem_len, xw // LANES, LANES))
        xp, hfp = _xattn_prompt(xp, norm_xattn[l], norm_ffn[l], wq_all, wo_all, l, kp, vp, batch=bp, seq=tp, tq=256,
                                name="xattn_p")
        xs, hfs = _xattn_sample(xs, norm_xattn[l], norm_ffn[l], wq_all, wo_all, cache_mem_k, cache_mem_v, l,
                                "xattn_s")
        nxt = [] if l + 1 == depth else [(w_in_c, (l + 1) // 2)] if l % 2 == 0 else [(w_in_ab, (l + 1) // 2)]
        mid, w2, *w_next = _matmul_swiglu(hfp, w13, tm=tm_p, tn=ffn_tn, rounding=[(ffn_w2, l)] + nxt, name="ffn13_p")
        if w_next:
            w_in = w_next[0]
        xp = _matmul([mid], w2, out_dtype=F32, tm=512, tn=1024, resid=xp, w_single=True, name="ffn2_p")
        mid = _matmul_swiglu(hfs, w13, tm=tm_s, tn=ffn_tn, name="ffn13_s")
        xs = _matmul([mid], w2, out_dtype=F32, tm=tm_s, tn=512, resid=xs, name="ffn2_s")
    y_prompt = _rmsnorm(xp, norm_final, F32, "norm_final_p").reshape(bp, tp, d)
    y_sample = _rmsnorm(xs, norm_final, F32, "norm_final_s").reshape(bd, ts, d)
    return (y_prompt, y_sample, sbk_p, sbv_p,
            jnp.stack(sbk_s, axis=1), jnp.stack(sbv_s, axis=1),
            jnp.stack(gv_s, axis=1),
            jnp.stack(hs_p, axis=0), jnp.stack(hs_s, axis=0),
            jnp.stack(mk_p, axis=0), jnp.stack(mv_p, axis=0))
```

```python
import functools

import numpy as np

import jax
import jax.numpy as jnp
from jax import lax
from jax.experimental import pallas as pl
from jax.experimental.pallas import tpu as pltpu

F32 = jnp.float32
BF16 = jnp.bfloat16
EPS = 1e-6
LANES = 128
SUBLANES = 8
MXU_WIDTH = 256
VMEM_CAP = 60 * 1024 * 1024
VMEM_SLACK = 12 * 1024 * 1024
NT_DIMS = (((1,), (1,)), ((), ()))
TN_DIMS = (((0,), (0,)), ((), ()))


def _nbytes(shape, dtype):
    n = 1
    for s in shape:
        n *= s
    return n * jnp.dtype(dtype).itemsize


def _params(semantics, blocks):
    need = sum(_nbytes(b[0], b[1]) * (b[2] if len(b) > 2 else 2) for b in blocks) + VMEM_SLACK
    return pltpu.CompilerParams(dimension_semantics=semantics,
                                vmem_limit_bytes=int(min(need, VMEM_CAP)))


def _rms_rows(x, g):
    r = lax.rsqrt(jnp.mean(x * x, axis=-1, keepdims=True) + EPS)
    return (x * r) * g


def _bf16_pieces(x, terms):
    out = []
    for _ in range(terms):
        piece = x.astype(BF16)
        out.append(piece)
        x = x - piece.astype(F32)
    return out


def _neg_softplus(z):
    return -(jnp.maximum(z, 0.0) + jnp.log(1.0 + jnp.exp(-jnp.abs(z))))


def _sum_over_later(log_keep, later, terms):
    w = later.shape[0]
    groups = [log_keep[:, g * w:(g + 1) * w] for g in range(log_keep.shape[1] // w)]
    inside = []
    for lk in groups:
        acc = None
        for piece in _bf16_pieces(lk, terms):
            part = jnp.dot(piece, later, preferred_element_type=F32)
            acc = part if acc is None else acc + part
        inside.append(acc)
    out, beyond = [], None
    for lk, acc in zip(reversed(groups), reversed(inside)):
        out.append(acc if beyond is None else acc + beyond)
        total = acc[:, :1] + lk[:, :1]
        beyond = total if beyond is None else beyond + total
    return jnp.concatenate(out[::-1], axis=1) if len(out) > 1 else out[0]


def _norm_body(x_ref, g_ref, o_ref):
    o_ref[...] = _rms_rows(x_ref[...], g_ref[...]).astype(o_ref.dtype)


def _rmsnorm(x, g, out_dtype, name):
    m, d = x.shape
    tm = min(m, 256)
    return pl.pallas_call(
        _norm_body,
        grid=(m // tm,),
        in_specs=[pl.BlockSpec((tm, d), lambda i: (i, 0)),
                  pl.BlockSpec((1, d), lambda i: (0, 0))],
        out_specs=pl.BlockSpec((tm, d), lambda i: (i, 0)),
        out_shape=jax.ShapeDtypeStruct((m, d), out_dtype),
        compiler_params=_params(("parallel",), [((tm, d), F32), ((tm, d), out_dtype)]),
        name=name,
    )(x, g.reshape(1, d))


def _w_spec(w, layer, k, tn, col_of, **mode):
    if w.ndim == 3:
        return pl.BlockSpec((None, k, tn), lambda j, i: (layer, 0, col_of(j)), **mode)
    return pl.BlockSpec((k, tn), lambda j, i: (0, col_of(j)), **mode)


BF16_ROWS = 16


def _rounding_specs(jobs, n_steps, step_of):
    in_specs, out_specs, out_shape, blocks = [], [], [], []
    for w, layer in jobs:
        rows, cols = w.shape[-2:]
        r = BF16_ROWS
        while rows % r or rows // r > n_steps:
            r += BF16_ROWS
        last = rows // r - 1
        in_specs.append(pl.BlockSpec((None, r, cols),
                                     lambda *g, layer=layer, last=last: (layer, jnp.minimum(step_of(*g), last), 0)))
        out_specs.append(pl.BlockSpec((r, cols), lambda *g, last=last: (jnp.minimum(step_of(*g), last), 0)))
        out_shape.append(jax.ShapeDtypeStruct((rows, cols), BF16))
        blocks += [((r, cols), F32), ((r, cols), BF16)]
    return in_specs, out_specs, out_shape, blocks


def _round_blocks(src_refs, dst_refs):
    for src, dst in zip(src_refs, dst_refs):
        dst[...] = src[...].astype(BF16)


def _mm_body(*refs, k_sizes, has_resid):
    n_a = len(k_sizes)
    a_refs, w_ref, o_ref = refs[:n_a], refs[n_a], refs[-1]
    acc, off = None, 0
    for a_ref, kk in zip(a_refs, k_sizes):
        part = jnp.dot(a_ref[...], w_ref[off:off + kk, :], preferred_element_type=F32)
        acc = part if acc is None else acc + part
        off += kk
    if has_resid:
        acc = acc + refs[n_a + 1][...]
    o_ref[...] = acc.astype(o_ref.dtype)


def _matmul(a_list, w, *, out_dtype, tm, tn, layer=None, resid=None, w_cols=None, w_single=False, name):
    m = a_list[0].shape[0]
    k_sizes = tuple(a.shape[1] for a in a_list)
    k = w.shape[-2]
    first, n = w_cols if w_cols is not None else (0, w.shape[-1])
    assert sum(k_sizes) == k and m % tm == 0 and n % tn == 0 and first % tn == 0
    j0 = first // tn
    in_specs = [pl.BlockSpec((tm, kk), lambda j, i: (i, 0)) for kk in k_sizes]
    w_mode = dict(pipeline_mode=pl.Buffered(1)) if w_single else {}
    in_specs.append(_w_spec(w, layer, k, tn, lambda j: j0 + j, **w_mode))
    blocks = [((tm, kk), BF16) for kk in k_sizes] + [((k, tn), w.dtype, 1 if w_single else 2), ((tm, tn), out_dtype)]
    args = list(a_list) + [w]
    if resid is not None:
        in_specs.append(pl.BlockSpec((tm, tn), lambda j, i: (i, j)))
        blocks.append(((tm, tn), F32))
        args.append(resid)
    return pl.pallas_call(
        functools.partial(_mm_body, k_sizes=k_sizes, has_resid=resid is not None),
        grid=(n // tn, m // tm),
        in_specs=in_specs,
        out_specs=pl.BlockSpec((tm, tn), lambda j, i: (i, j)),
        out_shape=jax.ShapeDtypeStruct((m, n), out_dtype),
        compiler_params=_params(("parallel", "parallel"), blocks),
        name=name,
    )(*args)


def _mm_heads_body(a_ref, w_ref, *rest, hb):
    o2_ref, o5_ref = rest[-2], rest[-1]
    acc = jnp.dot(a_ref[...], w_ref[...], preferred_element_type=F32)
    o2_ref[...] = acc.astype(o2_ref.dtype)
    o5_ref[...] = acc.reshape(acc.shape[0], hb, LANES)


def _matmul_heads(a, w, *, layer, w_cols, batch, seq, slot, n_slots, stacked, tm, name):
    m, k = a.shape
    first, n = w_cols
    heads = n // LANES
    hb = SUBLANES
    tn = hb * LANES
    tps = seq // tm
    assert m == batch * seq and seq % tm == 0 and n % tn == 0 and first % tn == 0
    j0 = first // tn
    in_specs = [pl.BlockSpec((tm, k), lambda j, i: (i, 0)),
                _w_spec(w, layer, k, tn, lambda j: j0 + j)]
    args = [a, w]
    aliases = {}
    if stacked is not None:
        in_specs.append(pl.BlockSpec(memory_space=pl.ANY))
        args.append(stacked)
        aliases = {2: 1}
    return pl.pallas_call(
        functools.partial(_mm_heads_body, hb=hb),
        grid=(n // tn, m // tm),
        in_specs=in_specs,
        out_specs=[pl.BlockSpec((tm, tn), lambda j, i: (i, j)),
                   pl.BlockSpec((None, None, tm, hb, LANES), lambda j, i: (i // tps, slot, i % tps, j, 0))],
        out_shape=[jax.ShapeDtypeStruct((m, n), BF16),
                   jax.ShapeDtypeStruct((batch, n_slots, seq, heads, LANES), F32)],
        input_output_aliases=aliases,
        compiler_params=_params(("parallel", "parallel"),
                                [((tm, k), BF16), ((k, tn), BF16), ((tm, tn), BF16), ((tm, tn), F32)]),
        name=name,
    )(*args)


def _swiglu_body(a_ref, wa_ref, wb_ref, *refs):
    n_round = (len(refs) - 1) // 2
    o_ref = refs[n_round]
    a = a_ref[...]
    ga = jnp.dot(a, wa_ref[...], preferred_element_type=F32)
    gb = jnp.dot(a, wb_ref[...], preferred_element_type=F32)
    o_ref[...] = (jax.nn.silu(ga) * gb).astype(o_ref.dtype)
    _round_blocks(refs[:n_round], refs[n_round + 1:])


def _matmul_swiglu(a, w13, *, tm, tn, rounding=(), name):
    m, k = a.shape
    hidden = w13.shape[1] // 2
    assert m % tm == 0 and hidden % tn == 0
    nj = hidden // tn
    r_in, r_out, r_shape, r_blocks = _rounding_specs(rounding, (m // tm) * nj, lambda i, j: i * nj + j)
    res = pl.pallas_call(
        _swiglu_body,
        grid=(m // tm, nj),
        in_specs=[pl.BlockSpec((tm, k), lambda i, j: (i, 0)),
                  pl.BlockSpec((k, tn), lambda i, j: (0, j)),
                  pl.BlockSpec((k, tn), lambda i, j: (0, j + nj))] + r_in,
        out_specs=[pl.BlockSpec((tm, tn), lambda i, j: (i, j))] + r_out,
        out_shape=[jax.ShapeDtypeStruct((m, hidden), BF16)] + r_shape,
        compiler_params=_params(("arbitrary", "arbitrary"),
                                [((tm, k), BF16), ((k, tn), BF16), ((k, tn), BF16), ((tm, tn), BF16)] + r_blocks),
        name=name,
    )(a, w13, w13, *[w for w, _ in rounding])
    return res[0] if not rounding else tuple(res)


def _sb_prompt_body(bias_ref, q_ref, k_ref, v_ref, *refs, tq, hp, scale):
    n_round = (len(refs) - 1) // 2
    o_ref = refs[n_round]
    _round_blocks(refs[:n_round], refs[n_round + 1:])
    qi = pl.program_id(2)
    d = LANES
    row = lax.broadcasted_iota(jnp.int32, (tq, tq), 0)
    col = lax.broadcasted_iota(jnp.int32, (tq, tq), 1)
    causal = col < row
    gw = min(tq, MXU_WIDTH)
    later = (lax.broadcasted_iota(jnp.int32, (gw, gw), 0)
             > lax.broadcasted_iota(jnp.int32, (gw, gw), 1)).astype(BF16)

    def block(kb, h, acc, carry, mask):
        hs = slice(h * d, (h + 1) * d)
        ks = k_ref[pl.ds(kb * tq, tq), hs]
        vs = v_ref[pl.ds(kb * tq, tq), hs]
        z = lax.dot_general(q_ref[:, hs], ks, NT_DIMS, preferred_element_type=F32)
        z = z * scale + bias_ref[h]
        log_keep = _neg_softplus(z)
        if mask:
            log_keep = jnp.where(causal, log_keep, 0.0)
        after = _sum_over_later(log_keep, later, 2)
        w = jnp.exp(log_keep + z + after + carry)
        if mask:
            w = jnp.where(causal, w, 0.0)
        acc = acc + jnp.dot(w.astype(BF16), vs, preferred_element_type=F32)
        carry = carry + after[:, :1] + log_keep[:, :1]
        return acc, carry

    state = []
    for h in range(hp):
        state.extend(block(qi, h, jnp.zeros((tq, d), F32), jnp.zeros((tq, 1), F32), True))

    def body(it, c):
        out = []
        for h in range(hp):
            out.extend(block(qi - 1 - it, h, c[2 * h], c[2 * h + 1], False))
        return tuple(out)

    state = lax.fori_loop(0, qi, body, tuple(state))
    for h in range(hp):
        o_ref[:, h * d:(h + 1) * d] = state[2 * h].astype(o_ref.dtype)


def _sb_prompt(q, k, v, bias, *, batch, seq, heads, tq, hp, rounding=(), name):
    d = LANES
    nq = seq // tq
    ng = heads // hp
    bias_rows = jnp.broadcast_to(bias.astype(F32)[:, None, None], (heads, 1, tq))
    q_spec = pl.BlockSpec((tq, hp * d), lambda b, h, i: (b * nq + i, h))
    kv_spec = pl.BlockSpec((seq, hp * d), lambda b, h, i: (b, h))
    r_in, r_out, r_shape, r_blocks = _rounding_specs(rounding, batch * ng * nq,
                                                     lambda b, h, i: (b * ng + h) * nq + i)
    res = pl.pallas_call(
        functools.partial(_sb_prompt_body, tq=tq, hp=hp, scale=d ** -0.5),
        grid=(batch, ng, nq),
        in_specs=[pl.BlockSpec((hp, 1, tq), lambda b, h, i: (h, 0, 0)), q_spec, kv_spec, kv_spec] + r_in,
        out_specs=[q_spec] + r_out,
        out_shape=[jax.ShapeDtypeStruct((batch * seq, heads * d), BF16)] + r_shape,
        compiler_params=_params(("arbitrary", "arbitrary", "arbitrary"),
                                [((tq, hp * d), BF16)] * 2 + [((seq, hp * d), BF16)] * 2 + r_blocks),
        name=name,
    )(bias_rows, q, k, v, *[w for w, _ in rounding])
    return res[0] if not rounding else tuple(res)


def _sb_sample_body(pt_ref, q_ref, bias_ref, *refs, heads, pps, scale):
    del pt_ref
    k_refs, v_refs = refs[:pps], refs[pps:2 * pps]
    o_ref, acc_ref, carry_ref = refs[2 * pps:]
    pg = pl.program_id(1)
    page = k_refs[0].shape[0]
    d = LANES

    @pl.when(pg == 0)
    def _():
        acc_ref[...] = jnp.zeros_like(acc_ref)
        carry_ref[...] = jnp.zeros_like(carry_ref)

    row = lax.broadcasted_iota(jnp.int32, (page, page), 0)
    col = lax.broadcasted_iota(jnp.int32, (page, page), 1)
    later = (row > col).astype(BF16)
    hrow = lax.broadcasted_iota(jnp.int32, (heads, page), 0)
    hrow_d = lax.broadcasted_iota(jnp.int32, (heads, d), 0)
    q = q_ref[...]
    bias = bias_ref[...]
    acc = acc_ref[...]
    carry = carry_ref[...]
    for k_ref, v_ref in zip(k_refs, v_refs):
        kt = jnp.swapaxes(k_ref[...], 0, 1).astype(BF16)
        vt = jnp.swapaxes(v_ref[...], 0, 1).astype(BF16)
        z = jnp.zeros((heads, page), F32)
        for h in range(heads):
            zh = lax.dot_general(q, kt[h], NT_DIMS, preferred_element_type=F32)
            z = jnp.where(hrow == h, zh, z)
        z = z * scale + bias
        log_keep = _neg_softplus(z)
        after = _sum_over_later(log_keep, later, 3)
        w = jnp.exp(log_keep + z + after + carry).astype(BF16)
        carry = carry + after[:, :1] + log_keep[:, :1]
        for h in range(heads):
            oh = jnp.dot(w, vt[h], preferred_element_type=F32)
            acc = acc + jnp.where(hrow_d == h, oh, 0.0)
    acc_ref[...] = acc
    carry_ref[...] = carry

    @pl.when(pg == pl.num_programs(1) - 1)
    def _():
        o_ref[...] = acc.astype(o_ref.dtype)


def _sb_sample(q, bias, cache_k, cache_v, page_table, layer, *, pps, name):
    bsz, heads, d = q.shape
    page = cache_k.shape[2]
    n_pages = page_table.shape[1]
    assert n_pages % pps == 0
    bias_rows = jnp.broadcast_to(bias.astype(F32)[:, None], (heads, page))

    def cache_spec(r):
        return pl.BlockSpec((None, None, page, heads, d),
                            lambda b, p, pt: (pt[b, n_pages - 1 - (p * pps + r)], layer, 0, 0, 0))

    grid_spec = pltpu.PrefetchScalarGridSpec(
        num_scalar_prefetch=1,
        grid=(bsz, n_pages // pps),
        in_specs=[pl.BlockSpec((None, heads, d), lambda b, p, pt: (b, 0, 0)),
                  pl.BlockSpec((heads, page), lambda b, p, pt: (0, 0))]
                 + [cache_spec(r) for r in range(pps)] * 2,
        out_specs=pl.BlockSpec((None, heads, d), lambda b, p, pt: (b, 0, 0)),
        scratch_shapes=[pltpu.VMEM((heads, d), F32), pltpu.VMEM((heads, 1), F32)],
    )
    out = pl.pallas_call(
        functools.partial(_sb_sample_body, heads=heads, pps=pps, scale=d ** -0.5),
        grid_spec=grid_spec,
        out_shape=jax.ShapeDtypeStruct((bsz, heads, d), BF16),
        compiler_params=_params(("parallel", "arbitrary"), [((page, heads, d), F32)] * (2 * pps)),
        name=name,
    )(page_table, q.astype(BF16), bias_rows, *([cache_k] * pps), *([cache_v] * pps))
    return out.reshape(bsz, heads * d)


def _gmlp_prompt_body(u_ref, vb_ref, gn_ref, ws_ref, bst_ref, o_ref, *, groups):
    chunk = u_ref.shape[0]
    vn = _rms_rows(jax.nn.gelu(vb_ref[...]), gn_ref[...])
    row = lax.broadcasted_iota(jnp.int32, (chunk, chunk), 0)
    col = lax.broadcasted_iota(jnp.int32, (chunk, chunk), 1)
    for g in range(groups):
        sl = slice(g * LANES, (g + 1) * LANES)
        w = jnp.where(col <= row, ws_ref[g], 0.0).astype(BF16)
        s = jnp.dot(w, vn[:, sl].astype(BF16), preferred_element_type=F32) + bst_ref[:, g:g + 1]
        o_ref[:, sl] = (jax.nn.gelu(u_ref[:, sl]) * s).astype(o_ref.dtype)


def _gmlp_prompt(p, gnorm, ws, bs, *, u_col, name):
    groups, chunk, _ = ws.shape
    width = groups * LANES
    m = p.shape[0]
    return pl.pallas_call(
        functools.partial(_gmlp_prompt_body, groups=groups),
        grid=(m // chunk,),
        in_specs=[pl.BlockSpec((chunk, width), lambda i: (i, u_col)),
                  pl.BlockSpec((chunk, width), lambda i: (i, u_col + 1)),
                  pl.BlockSpec((1, width), lambda i: (0, 0)),
                  pl.BlockSpec((groups, chunk, chunk), lambda i: (0, 0, 0)),
                  pl.BlockSpec((chunk, groups), lambda i: (0, 0))],
        out_specs=pl.BlockSpec((chunk, width), lambda i: (i, 0)),
        out_shape=jax.ShapeDtypeStruct((m, width), BF16),
        compiler_params=_params(("parallel",),
                                [((chunk, width), F32)] * 2 + [((groups, chunk, chunk), F32), ((chunk, width), BF16)]),
        name=name,
    )(p, p, gnorm.reshape(1, width), ws, bs.T)


def _gmlp_sample_body(u_ref, vb_ref, gn_ref, w0_ref, b0_ref, o_ref, vn_ref):
    vn = _rms_rows(jax.nn.gelu(vb_ref[...]), gn_ref[...])
    vn_ref[...] = vn
    o_ref[...] = (jax.nn.gelu(u_ref[...]) * (w0_ref[...] * vn + b0_ref[...])).astype(o_ref.dtype)


def _gmlp_sample(p, gnorm, ws, bs, *, u_col, name):
    groups = ws.shape[0]
    width = groups * LANES
    m = p.shape[0]
    w0 = jnp.repeat(ws[:, 0, 0], LANES).reshape(1, width)
    b0 = jnp.repeat(bs[:, 0], LANES).reshape(1, width)
    row = lambda c: pl.BlockSpec((m, width), lambda i: (0, c))
    vec = pl.BlockSpec((1, width), lambda i: (0, 0))
    return pl.pallas_call(
        _gmlp_sample_body,
        grid=(1,),
        in_specs=[row(u_col), row(u_col + 1), vec, vec, vec],
        out_specs=[row(0), row(0)],
        out_shape=[jax.ShapeDtypeStruct((m, width), BF16), jax.ShapeDtypeStruct((m, width), F32)],
        compiler_params=_params(("arbitrary",), [((m, width), F32)] * 4),
        name=name,
    )(p, p, gnorm.reshape(1, width), w0, b0)


HGRN_LEVELS = 7


def _hgrn_gates(fpre, lb):
    t = jnp.log(1.0 + jnp.exp(-jnp.abs(fpre)))
    log_sig = jnp.minimum(fpre, 0.0) - t
    log_sig_neg = jnp.minimum(-fpre, 0.0) - t
    a = jnp.log(lb)
    c = jnp.log1p(-lb) + log_sig
    logf = jnp.maximum(a, c) + jnp.log(1.0 + jnp.exp(-jnp.abs(a - c)))
    return logf, (1.0 - lb) * jnp.exp(log_sig_neg)


def _hgrn_finish(o, g, gn):
    return _rms_rows(o, gn) * jax.nn.silu(g)


def _hgrn_tables():
    n = LANES
    t = np.arange(n)[:, None]
    r = np.arange(n)[None, :]
    blocks = []
    for level in range(HGRN_LEVELS):
        c = 1 << level
        mid = (t // (2 * c)) * (2 * c) + c
        second = (t % (2 * c)) >= c
        blocks.append(np.where(second, (r >= mid) & (r <= t), (r > t) & (r < mid)))
    blocks += [r <= t, r > t, np.ones((SUBLANES, n), bool)]
    ranges = np.concatenate(blocks, axis=0).astype(np.float32)
    diff = t ^ r
    level_of = np.where(t == r, -1, np.where(r < t, np.floor(np.log2(np.maximum(diff, 1))), HGRN_LEVELS))
    return (jnp.asarray(np.concatenate([ranges, ranges], axis=1), BF16),
            jnp.asarray(level_of.astype(np.int32)))


def _hgrn_prompt_body(q_ref, f_ref, i_ref, g_ref, lb_ref, gn_ref, rng_ref, lvl_ref, *refs, tb):
    n_round = (len(refs) - 3) // 2
    o_ref, s_ref = refs[n_round:n_round + 2]
    st_ref = refs[-1]
    _round_blocks(refs[:n_round], refs[n_round + 2:-1])
    step = pl.program_id(2)
    n = LANES

    @pl.when(step == 0)
    def _():
        st_ref[...] = jnp.zeros_like(st_ref)

    lb = lb_ref[...]
    gn = gn_ref[...]
    level_of = lvl_ref[...]
    st = st_ref[...]
    tiles = [slice(i * n, (i + 1) * n) for i in range(tb // n)]
    logf_all, k_all = _hgrn_gates(f_ref[...], lb)
    rhs = jnp.concatenate([jnp.concatenate([piece[rs] for rs in tiles], axis=1)
                           for piece in _bf16_pieces(logf_all, 2)], axis=0)
    decay_all = jnp.exp(jnp.dot(rng_ref[...], rhs, preferred_element_type=F32))
    for rs in tiles:
        q = q_ref[rs, :]
        v = i_ref[rs, :]
        k = k_all[rs]
        decay = decay_all[:, rs]
        qb, kb, vb = q.astype(BF16), k.astype(BF16), v.astype(BF16)
        scores = jnp.zeros((n, n), F32)
        for level in range(HGRN_LEVELS):
            dl = decay[level * n:(level + 1) * n].astype(BF16)
            pair = lax.dot_general(qb * dl, kb * dl, NT_DIMS, preferred_element_type=F32)
            scores = jnp.where(level_of == level, pair, scores)
        same = lax.dot_general(qb, kb, NT_DIMS, preferred_element_type=F32)
        scores = jnp.where(level_of == -1, same, scores)
        from_start = decay[HGRN_LEVELS * n:(HGRN_LEVELS + 1) * n].astype(BF16)
        to_end = decay[(HGRN_LEVELS + 1) * n:(HGRN_LEVELS + 2) * n].astype(BF16)
        total = decay[(HGRN_LEVELS + 2) * n:(HGRN_LEVELS + 2) * n + 1]
        o = (jnp.dot(scores.astype(BF16), vb, preferred_element_type=F32)
             + lax.dot_general(qb * from_start, st.astype(BF16), NT_DIMS, preferred_element_type=F32))
        st = st * total + lax.dot_general(vb, kb * to_end, TN_DIMS, preferred_element_type=F32)
        o_ref[rs, :] = _hgrn_finish(o, g_ref[rs, :], gn).astype(o_ref.dtype)
    st_ref[...] = st

    @pl.when(step == pl.num_programs(2) - 1)
    def _():
        s_ref[...] = st.T


def _hgrn_prompt(p, lb, gnorm, *, batch, seq, heads, tb, rounding=(), name):
    d = LANES
    nt = seq // tb
    ranges, level_of = _hgrn_tables()
    col = lambda c: pl.BlockSpec((tb, d), lambda b, h, t: (b * nt + t, c * heads + h))
    r_in, r_out, r_shape, r_blocks = _rounding_specs(rounding, batch * heads * nt,
                                                     lambda b, h, t: (b * heads + h) * nt + t)
    res = pl.pallas_call(
        functools.partial(_hgrn_prompt_body, tb=tb),
        grid=(batch, heads, nt),
        in_specs=[col(0), col(1), col(2), col(3),
                  pl.BlockSpec((None, 1, d), lambda b, h, t: (h, 0, 0)),
                  pl.BlockSpec((1, d), lambda b, h, t: (0, 0)),
                  pl.BlockSpec(ranges.shape, lambda b, h, t: (0, 0)),
                  pl.BlockSpec(level_of.shape, lambda b, h, t: (0, 0))] + r_in,
        out_specs=[pl.BlockSpec((tb, d), lambda b, h, t: (b * nt + t, h)),
                   pl.BlockSpec((None, None, d, d), lambda b, h, t: (b, h, 0, 0))] + r_out,
        out_shape=[jax.ShapeDtypeStruct((batch * seq, heads * d), BF16),
                   jax.ShapeDtypeStruct((batch, heads, d, d), F32)] + r_shape,
        scratch_shapes=[pltpu.VMEM((d, d), F32)],
        compiler_params=_params(("arbitrary", "arbitrary", "arbitrary"),
                                [((tb, d), F32)] * 5 + [((d, d), F32), (ranges.shape, BF16)] + r_blocks),
        name=name,
    )(p, p, p, p, lb.reshape(heads, 1, d), gnorm.reshape(1, d), ranges, level_of, *[w for w, _ in rounding])
    return tuple(res)


def _hgrn_sample_body(q_ref, f_ref, i_ref, g_ref, lb_ref, gn_ref, s_ref, o_ref, so_ref, oacc_ref, *, heads):
    d = LANES
    q = q_ref[...]
    v = i_ref[...]
    logf, k = _hgrn_gates(f_ref[...], lb_ref[...])
    f = jnp.exp(logf)
    eye = lax.broadcasted_iota(jnp.int32, (d, d), 0) == lax.broadcasted_iota(jnp.int32, (d, d), 1)

    def to_col(r):
        return jnp.sum(jnp.where(eye, r, 0.0), axis=1, keepdims=True)

    for h in range(heads):
        hs = slice(h, h + 1)
        s_new = to_col(f[hs]) * s_ref[h] + to_col(k[hs]) * v[hs]
        so_ref[h] = s_new
        oacc_ref[hs, :] = jnp.sum(to_col(q[hs]) * s_new, axis=0, keepdims=True)
    o_ref[...] = _hgrn_finish(oacc_ref[...], g_ref[...], gn_ref[...]).astype(o_ref.dtype)


def _hgrn_sample(p, lb, gnorm, state, layer, *, heads, name):
    bsz = p.shape[0]
    d = LANES
    p4 = p.reshape(bsz, 4, heads, d)
    part = lambda c: pl.BlockSpec((None, None, heads, d), lambda b: (b, c, 0, 0))
    o, s_new = pl.pallas_call(
        functools.partial(_hgrn_sample_body, heads=heads),
        grid=(bsz,),
        in_specs=[part(0), part(1), part(2), part(3),
                  pl.BlockSpec((heads, d), lambda b: (0, 0)),
                  pl.BlockSpec((1, d), lambda b: (0, 0)),
                  pl.BlockSpec((None, None, heads, d, d), lambda b: (layer, b, 0, 0, 0))],
        out_specs=[pl.BlockSpec((None, heads, d), lambda b: (b, 0, 0)),
                   pl.BlockSpec((None, heads, d, d), lambda b: (b, 0, 0, 0))],
        out_shape=[jax.ShapeDtypeStruct((bsz, heads, d), BF16),
                   jax.ShapeDtypeStruct((bsz, heads, d, d), F32)],
        scratch_shapes=[pltpu.VMEM((heads, d), F32)],
        compiler_params=_params(("parallel",), [((heads, d, d), F32)] * 2),
        name=name,
    )(p4, p4, p4, p4, lb.reshape(heads, d), gnorm.reshape(1, d), state)
    return o.reshape(bsz, heads * d), s_new


def _attend(q, k_ref, v_ref, heads, scale):
    outs = []
    for h in range(heads):
        sl = slice(h * LANES, (h + 1) * LANES)
        s = lax.dot_general(q[:, sl].astype(BF16), k_ref[:, sl].astype(BF16), NT_DIMS,
                            preferred_element_type=F32) * scale
        e = jnp.exp(s - jnp.max(s, axis=-1, keepdims=True))
        prob = e * (1.0 / jnp.sum(e, axis=-1, keepdims=True))
        outs.append(jnp.dot(prob.astype(BF16), v_ref[:, sl].astype(BF16), preferred_element_type=F32))
    return jnp.concatenate(outs, axis=1)


def _xattn_prompt_body(x_ref, gx_ref, gf_ref, wq_ref, wo_ref, k_ref, v_ref, y_ref, hf_ref, *, heads, scale):
    x = x_ref[...]
    h = _rms_rows(x, gx_ref[...]).astype(BF16)
    q = jnp.dot(h, wq_ref[...], preferred_element_type=F32)
    att = _attend(q, k_ref, v_ref, heads, scale).astype(BF16)
    y = x + jnp.dot(att, wo_ref[...], preferred_element_type=F32)
    y_ref[...] = y
    hf_ref[...] = _rms_rows(y, gf_ref[...]).astype(hf_ref.dtype)


def _xattn_prompt(x, gx, gf, wq, wo, layer, mem_k, mem_v, *, batch, seq, tq, name):
    d = x.shape[1]
    xw = wq.shape[2]
    heads = xw // LANES
    mlen = mem_k.shape[1]
    nt = seq // tq
    rows = pl.BlockSpec((tq, d), lambda b, t: (b * nt + t, 0))
    vec = pl.BlockSpec((1, d), lambda b, t: (0, 0))
    mem = pl.BlockSpec((None, mlen, xw), lambda b, t: (b, 0, 0))
    return pl.pallas_call(
        functools.partial(_xattn_prompt_body, heads=heads, scale=LANES ** -0.5),
        grid=(batch, nt),
        in_specs=[rows, vec, vec,
                  pl.BlockSpec((None, d, xw), lambda b, t: (layer, 0, 0)),
                  pl.BlockSpec((None, xw, d), lambda b, t: (layer, 0, 0)),
                  mem, mem],
        out_specs=[rows, rows],
        out_shape=[jax.ShapeDtypeStruct(x.shape, F32), jax.ShapeDtypeStruct(x.shape, BF16)],
        compiler_params=_params(("parallel", "parallel"),
                                [((tq, d), F32)] * 2 + [((tq, d), BF16), ((d, xw), BF16), ((xw, d), BF16),
                                                        ((mlen, xw), F32), ((mlen, xw), F32)]),
        name=name,
    )(x, gx.reshape(1, d), gf.reshape(1, d), wq, wo, mem_k, mem_v)


def _xattn_sample_body(x_ref, gx_ref, gf_ref, wq_ref, wo_ref, k_ref, v_ref, y_ref, hf_ref, q_sc, att_sc,
                       *, heads, scale):
    b = pl.program_id(0)

    @pl.when(b == 0)
    def _():
        h = _rms_rows(x_ref[...], gx_ref[...]).astype(BF16)
        q_sc[...] = jnp.dot(h, wq_ref[...], preferred_element_type=F32)
        att_sc[...] = jnp.zeros_like(att_sc)

    att = _attend(q_sc[...], k_ref, v_ref, heads, scale)
    mine = lax.broadcasted_iota(jnp.int32, att.shape, 0) == b
    att_sc[...] += jnp.where(mine, att, 0.0)

    @pl.when(b == pl.num_programs(0) - 1)
    def _():
        y = x_ref[...] + jnp.dot(att_sc[...].astype(BF16), wo_ref[...], preferred_element_type=F32)
        y_ref[...] = y
        hf_ref[...] = _rms_rows(y, gf_ref[...]).astype(hf_ref.dtype)


def _xattn_sample(x, gx, gf, wq, wo, mem_k, mem_v, layer, name):
    bsz, d = x.shape
    xw = wq.shape[2]
    heads = xw // LANES
    mlen = mem_k.shape[2]
    mk = mem_k.reshape(mem_k.shape[0], bsz, mlen, xw)
    mv = mem_v.reshape(mem_v.shape[0], bsz, mlen, xw)
    rows = pl.BlockSpec((bsz, d), lambda b: (0, 0))
    vec = pl.BlockSpec((1, d), lambda b: (0, 0))
    mem = pl.BlockSpec((None, None, mlen, xw), lambda b: (layer, b, 0, 0))
    return pl.pallas_call(
        functools.partial(_xattn_sample_body, heads=heads, scale=LANES ** -0.5),
        grid=(bsz,),
        in_specs=[rows, vec, vec,
                  pl.BlockSpec((None, d, xw), lambda b: (layer, 0, 0)),
                  pl.BlockSpec((None, xw, d), lambda b: (layer, 0, 0)),
                  mem, mem],
        out_specs=[rows, rows],
        out_shape=[jax.ShapeDtypeStruct(x.shape, F32), jax.ShapeDtypeStruct(x.shape, BF16)],
        scratch_shapes=[pltpu.VMEM((bsz, xw), F32), pltpu.VMEM((bsz, xw), F32)],
        compiler_params=_params(("arbitrary",),
                                [((d, xw), BF16), ((xw, d), BF16), ((mlen, xw), F32), ((mlen, xw), F32)]),
        name=name,
    )(x, gx.reshape(1, d), gf.reshape(1, d), wq, wo, mk, mv)


def kernel(x_prompt, x_sample, mem_prompt, cache_sb_k, cache_sb_v, cache_mem_k, cache_mem_v, state_hgrn, page_table, norm_mix, norm_mem, norm_xattn, norm_ffn, norm_final, w_in_ab, w_out_ab, sb_bias, gmlp_norm, gmlp_ws, gmlp_bs, w_in_c, w_out_c, hgrn_lb, hgrn_gnorm, xattn_wq, xattn_wkv, xattn_wo, ffn_w13, ffn_w2):
    bp, tp, d = x_prompt.shape
    bd, ts, _ = x_sample.shape
    assert ts == 1, "the sample group carries one new token per sequence"
    depth = norm_mix.shape[0]
    n_even = w_in_ab.shape[0]
    a_heads = sb_bias.shape[1]
    a_width = a_heads * LANES
    groups = gmlp_ws.shape[1]
    b_width = groups * LANES
    c_heads = hgrn_lb.shape[1] // LANES
    mem_len = mem_prompt.shape[1]
    xw = xattn_wq.shape[2]
    assert a_width == b_width and w_in_ab.shape[2] == 3 * a_width + 2 * b_width

    xp = x_prompt.reshape(bp * tp, d)
    xs = x_sample.reshape(bd, d)
    mem = mem_prompt.reshape(bp * mem_len, d)

    lb_all = jnp.cumsum(jax.nn.softmax(hgrn_lb.astype(F32), axis=0), axis=0)
    lb_all = lb_all - lb_all[:1]

    tm_p, tm_s = 1024, bd
    ffn_tn = 256

    w_in = w_in_ab[0].astype(BF16)
    wq_all = xattn_wq.astype(BF16)
    wo_all = xattn_wo.astype(BF16)
    wkv_all = xattn_wkv.astype(BF16)

    sbk_p, sbv_p = None, None
    sbk_s, sbv_s, gv_s, hs_p, hs_s, mk_p, mv_p = [], [], [], [], [], [], []
    for l in range(depth):
        j = l // 2
        hp = _rmsnorm(xp, norm_mix[l], BF16, "norm_mix_p")
        hs = _rmsnorm(xs, norm_mix[l], BF16, "norm_mix_s")
        if l % 2 == 0:
            qp = _matmul([hp], w_in, w_cols=(0, a_width), out_dtype=BF16, tm=tm_p, tn=1024, name="in_q_p")
            kp2, sbk_p = _matmul_heads(hp, w_in, layer=None, w_cols=(a_width, a_width), batch=bp, seq=tp, slot=j,
                                       n_slots=n_even, stacked=sbk_p, tm=tm_p, name="in_k_p")
            vp2, sbv_p = _matmul_heads(hp, w_in, layer=None, w_cols=(2 * a_width, a_width), batch=bp, seq=tp,
                                       slot=j, n_slots=n_even, stacked=sbv_p, tm=tm_p, name="in_v_p")
            uv = _matmul([hp], w_in, w_cols=(3 * a_width, 2 * b_width), out_dtype=F32, tm=tm_p, tn=1024,
                         name="in_uv_p")
            oa, w13, w_out = _sb_prompt(qp, kp2, vp2, sb_bias[j], batch=bp, seq=tp, heads=a_heads, tq=512, hp=2,
                                        rounding=[(ffn_w13, l), (w_out_ab, j)], name="sb_p")
            ob = _gmlp_prompt(uv, gmlp_norm[j], gmlp_ws[j], gmlp_bs[j], u_col=0, name="gmlp_p")
            xp = _matmul([oa, ob], w_out, out_dtype=F32, tm=tm_p, tn=1024, resid=xp, name="out_ab_p")
            ps = _matmul([hs], w_in, out_dtype=F32, tm=tm_s, tn=1024, name="in_ab_s")
            oa = _sb_sample(ps[:, :a_width].reshape(bd, a_heads, LANES), sb_bias[j], cache_sb_k, cache_sb_v,
                            page_table, j, pps=4, name="sb_s")
            ob, vn = _gmlp_sample(ps, gmlp_norm[j], gmlp_ws[j], gmlp_bs[j], u_col=3 * a_width // b_width,
                                  name="gmlp_s")
            xs = _matmul([oa, ob], w_out, out_dtype=F32, tm=tm_s, tn=1024, resid=xs, name="out_ab_s")
            sbk_s.append(ps[:, a_width:2 * a_width].reshape(bd, ts, a_heads, LANES))
            sbv_s.append(ps[:, 2 * a_width:3 * a_width].reshape(bd, ts, a_heads, LANES))
            gv_s.append(vn.reshape(bd, ts, groups, LANES))
        else:
            pp = _matmul([hp], w_in, out_dtype=F32, tm=tm_p, tn=1024, name="in_c_p")
            oc, s_p, w13, w_out = _hgrn_prompt(pp, lb_all[j], hgrn_gnorm[j], batch=bp, seq=tp, heads=c_heads,
                                               tb=1024, rounding=[(ffn_w13, l), (w_out_c, j)], name="hgrn_p")
            xp = _matmul([oc], w_out, out_dtype=F32, tm=tm_p, tn=1024, resid=xp, name="out_c_p")
            hs_p.append(s_p)
            ps = _matmul([hs], w_in, out_dtype=F32, tm=tm_s, tn=1024, name="in_c_s")
            oc, s_s = _hgrn_sample(ps, lb_all[j], hgrn_gnorm[j], state_hgrn, j, heads=c_heads, name="hgrn_s")
            xs = _matmul([oc], w_out, out_dtype=F32, tm=tm_s, tn=1024, resid=xs, name="out_c_s")
            hs_s.append(s_s)
        hm = _rmsnorm(mem, norm_mem[l], BF16, "norm_mem")
        kv = _matmul([hm], wkv_all, layer=l, out_dtype=F32, tm=bp * mem_len, tn=512, name="mem_kv")
        kp = kv[:, :xw].reshape(bp, mem_len, xw)
        vp = kv[:, xw:].reshape(bp, mem_len, xw)
        mk_p.append(kp.reshape(bp, mem_len, xw // LANES, LANES))
        mv_p.append(vp.reshape(bp, mem_len, xw // LANES, LANES))
        xp, hfp = _xattn_prompt(xp, norm_xattn[l], norm_ffn[l], wq_all, wo_all, l, kp, vp, batch=bp, seq=tp, tq=256,
                                name="xattn_p")
        xs, hfs = _xattn_sample(xs, norm_xattn[l], norm_ffn[l], wq_all, wo_all, cache_mem_k, cache_mem_v, l,
                                "xattn_s")
        nxt = [] if l + 1 == depth else [(w_in_c, (l + 1) // 2)] if l % 2 == 0 else [(w_in_ab, (l + 1) // 2)]
        mid, w2, *w_next = _matmul_swiglu(hfp, w13, tm=2 * tm_p, tn=ffn_tn, rounding=[(ffn_w2, l)] + nxt,
                                          name="ffn13_p")
        if w_next:
            w_in = w_next[0]
        xp = _matmul([mid], w2, out_dtype=F32, tm=512, tn=1024, resid=xp, w_single=True, name="ffn2_p")
        mid = _matmul_swiglu(hfs, w13, tm=tm_s, tn=ffn_tn, name="ffn13_s")
        xs = _matmul([mid], w2, out_dtype=F32, tm=tm_s, tn=512, resid=xs, name="ffn2_s")
    y_prompt = _rmsnorm(xp, norm_final, F32, "norm_final_p").reshape(bp, tp, d)
    y_sample = _rmsnorm(xs, norm_final, F32, "norm_final_s").reshape(bd, ts, d)
    return (y_prompt, y_sample, sbk_p, sbv_p,
            jnp.stack(sbk_s, axis=1), jnp.stack(sbv_s, axis=1),
            jnp.stack(gv_s, axis=1),
            jnp.stack(hs_p, axis=0), jnp.stack(hs_s, axis=0),
            jnp.stack(mk_p, axis=0), jnp.stack(mv_p, axis=0))
```

```python
import functools

import numpy as np

import jax
import jax.numpy as jnp
from jax import lax
from jax.experimental import pallas as pl
from jax.experimental.pallas import tpu as pltpu

F32 = jnp.float32
BF16 = jnp.bfloat16
EPS = 1e-6
LANES = 128
SUBLANES = 8
MXU_WIDTH = 256
VMEM_CAP = 60 * 1024 * 1024
VMEM_SLACK = 12 * 1024 * 1024
NT_DIMS = (((1,), (1,)), ((), ()))
TN_DIMS = (((0,), (0,)), ((), ()))


def _nbytes(shape, dtype):
    n = 1
    for s in shape:
        n *= s
    return n * jnp.dtype(dtype).itemsize


def _params(semantics, blocks):
    need = sum(_nbytes(b[0], b[1]) * (b[2] if len(b) > 2 else 2) for b in blocks) + VMEM_SLACK
    return pltpu.CompilerParams(dimension_semantics=semantics,
                                vmem_limit_bytes=int(min(need, VMEM_CAP)))


def _rms_rows(x, g):
    r = lax.rsqrt(jnp.mean(x * x, axis=-1, keepdims=True) + EPS)
    return (x * r) * g


def _bf16_pieces(x, terms):
    out = []
    for _ in range(terms):
        piece = x.astype(BF16)
        out.append(piece)
        x = x - piece.astype(F32)
    return out


def _neg_softplus(z):
    return -(jnp.maximum(z, 0.0) + jnp.log(1.0 + jnp.exp(-jnp.abs(z))))


def _sum_over_later(log_keep, later, terms):
    w = later.shape[0]
    groups = [log_keep[:, g * w:(g + 1) * w] for g in range(log_keep.shape[1] // w)]
    inside = []
    for lk in groups:
        acc = None
        for piece in _bf16_pieces(lk, terms):
            part = jnp.dot(piece, later, preferred_element_type=F32)
            acc = part if acc is None else acc + part
        inside.append(acc)
    out, beyond = [], None
    for lk, acc in zip(reversed(groups), reversed(inside)):
        out.append(acc if beyond is None else acc + beyond)
        total = acc[:, :1] + lk[:, :1]
        beyond = total if beyond is None else beyond + total
    return jnp.concatenate(out[::-1], axis=1) if len(out) > 1 else out[0]


def _norm_body(x_ref, g_ref, o_ref):
    o_ref[...] = _rms_rows(x_ref[...], g_ref[...]).astype(o_ref.dtype)


def _rmsnorm(x, g, out_dtype, name):
    m, d = x.shape
    tm = min(m, 256)
    return pl.pallas_call(
        _norm_body,
        grid=(m // tm,),
        in_specs=[pl.BlockSpec((tm, d), lambda i: (i, 0)),
                  pl.BlockSpec((1, d), lambda i: (0, 0))],
        out_specs=pl.BlockSpec((tm, d), lambda i: (i, 0)),
        out_shape=jax.ShapeDtypeStruct((m, d), out_dtype),
        compiler_params=_params(("parallel",), [((tm, d), F32), ((tm, d), out_dtype)]),
        name=name,
    )(x, g.reshape(1, d))


def _w_spec(w, layer, k, tn, col_of, **mode):
    if w.ndim == 3:
        return pl.BlockSpec((None, k, tn), lambda j, i: (layer, 0, col_of(j)), **mode)
    return pl.BlockSpec((k, tn), lambda j, i: (0, col_of(j)), **mode)


BF16_ROWS = 16


def _rounding_specs(jobs, n_steps, step_of):
    in_specs, out_specs, out_shape, blocks = [], [], [], []
    for w, layer in jobs:
        rows, cols = w.shape[-2:]
        r = BF16_ROWS
        while rows % r or rows // r > n_steps:
            r += BF16_ROWS
        last = rows // r - 1
        in_specs.append(pl.BlockSpec((None, r, cols),
                                     lambda *g, layer=layer, last=last: (layer, jnp.minimum(step_of(*g), last), 0)))
        out_specs.append(pl.BlockSpec((r, cols), lambda *g, last=last: (jnp.minimum(step_of(*g), last), 0)))
        out_shape.append(jax.ShapeDtypeStruct((rows, cols), BF16))
        blocks += [((r, cols), F32), ((r, cols), BF16)]
    return in_specs, out_specs, out_shape, blocks


def _round_blocks(src_refs, dst_refs):
    for src, dst in zip(src_refs, dst_refs):
        dst[...] = src[...].astype(BF16)


def _project(a_refs, w_ref, k_sizes):
    acc, off = None, 0
    for a_ref, kk in zip(a_refs, k_sizes):
        part = jnp.dot(a_ref[...], w_ref[off:off + kk, :], preferred_element_type=F32)
        acc = part if acc is None else acc + part
        off += kk
    return acc


def _mm_body(*refs, k_sizes, has_resid, has_sample):
    n_a = len(k_sizes)
    a_refs, w_ref = refs[:n_a], refs[n_a]
    pos = n_a + 1
    r_ref = refs[pos] if has_resid else None
    pos += has_resid
    acc = _project(a_refs, w_ref, k_sizes)
    if has_resid:
        acc = acc + r_ref[...]
    o_ref = refs[-2] if has_sample else refs[-1]
    o_ref[...] = acc.astype(o_ref.dtype)
    if has_sample:
        s_refs = refs[pos:pos + n_a]
        rs_ref = refs[pos + n_a] if has_resid else None
        os_ref = refs[-1]

        @pl.when(pl.program_id(1) == 0)
        def _():
            acc_s = _project(s_refs, w_ref, k_sizes)
            if has_resid:
                acc_s = acc_s + rs_ref[...]
            os_ref[...] = acc_s.astype(os_ref.dtype)


def _matmul(a_list, w, *, out_dtype, tm, tn, layer=None, resid=None, w_cols=None, w_single=False,
            sample=None, name):
    m = a_list[0].shape[0]
    k_sizes = tuple(a.shape[1] for a in a_list)
    k = w.shape[-2]
    first, n = w_cols if w_cols is not None else (0, w.shape[-1])
    assert sum(k_sizes) == k and m % tm == 0 and n % tn == 0 and first % tn == 0
    j0 = first // tn
    in_specs = [pl.BlockSpec((tm, kk), lambda j, i: (i, 0)) for kk in k_sizes]
    w_mode = dict(pipeline_mode=pl.Buffered(1)) if w_single else {}
    in_specs.append(_w_spec(w, layer, k, tn, lambda j: j0 + j, **w_mode))
    blocks = [((tm, kk), BF16) for kk in k_sizes] + [((k, tn), w.dtype, 1 if w_single else 2), ((tm, tn), out_dtype)]
    args = list(a_list) + [w]
    if resid is not None:
        in_specs.append(pl.BlockSpec((tm, tn), lambda j, i: (i, j)))
        blocks.append(((tm, tn), F32))
        args.append(resid)
    out_specs = [pl.BlockSpec((tm, tn), lambda j, i: (i, j))]
    out_shape = [jax.ShapeDtypeStruct((m, n), out_dtype)]
    if sample is not None:
        a_s_list, resid_s = sample
        ms = a_s_list[0].shape[0]
        assert (resid_s is None) == (resid is None) and tuple(a.shape[1] for a in a_s_list) == k_sizes
        in_specs += [pl.BlockSpec((ms, kk), lambda j, i: (0, 0)) for kk in k_sizes]
        args += list(a_s_list)
        if resid_s is not None:
            in_specs.append(pl.BlockSpec((ms, tn), lambda j, i: (0, j)))
            args.append(resid_s)
        out_specs.append(pl.BlockSpec((ms, tn), lambda j, i: (0, j)))
        out_shape.append(jax.ShapeDtypeStruct((ms, n), out_dtype))
        blocks += [((ms, k), BF16), ((ms, tn), F32), ((ms, tn), F32)]
    res = pl.pallas_call(
        functools.partial(_mm_body, k_sizes=k_sizes, has_resid=resid is not None, has_sample=sample is not None),
        grid=(n // tn, m // tm),
        in_specs=in_specs,
        out_specs=out_specs,
        out_shape=out_shape,
        compiler_params=_params(("parallel", "arbitrary"), blocks),
        name=name,
    )(*args)
    return res[0] if sample is None else tuple(res)


def _mm_heads_body(a_ref, w_ref, *rest, hb, slot):
    o2_ref, o5_ref = rest[-2], rest[-1]
    acc = jnp.dot(a_ref[...], w_ref[...], preferred_element_type=F32)
    o2_ref[...] = acc.astype(o2_ref.dtype)
    heads = acc.reshape(acc.shape[0], hb, LANES)
    if len(o5_ref.shape) == 3:
        o5_ref[...] = heads
    else:
        for s in range(o5_ref.shape[0]):
            o5_ref[s] = heads if s == slot else jnp.zeros_like(heads)


def _matmul_heads(a, w, *, layer, w_cols, batch, seq, slot, n_slots, stacked, tm, name):
    m, k = a.shape
    first, n = w_cols
    heads = n // LANES
    hb = SUBLANES
    tn = hb * LANES
    tps = seq // tm
    assert m == batch * seq and seq % tm == 0 and n % tn == 0 and first % tn == 0
    j0 = first // tn
    in_specs = [pl.BlockSpec((tm, k), lambda j, i: (i, 0)),
                _w_spec(w, layer, k, tn, lambda j: j0 + j)]
    args = [a, w]
    if stacked is None:
        aliases = {}
        o5_spec = pl.BlockSpec((None, n_slots, tm, hb, LANES), lambda j, i: (i // tps, 0, i % tps, j, 0))
        o5_block = ((n_slots, tm, tn), F32)
    else:
        in_specs.append(pl.BlockSpec(memory_space=pl.ANY))
        args.append(stacked)
        aliases = {2: 1}
        o5_spec = pl.BlockSpec((None, None, tm, hb, LANES), lambda j, i: (i // tps, slot, i % tps, j, 0))
        o5_block = ((tm, tn), F32)
    return pl.pallas_call(
        functools.partial(_mm_heads_body, hb=hb, slot=slot),
        grid=(n // tn, m // tm),
        in_specs=in_specs,
        out_specs=[pl.BlockSpec((tm, tn), lambda j, i: (i, j)), o5_spec],
        out_shape=[jax.ShapeDtypeStruct((m, n), BF16),
                   jax.ShapeDtypeStruct((batch, n_slots, seq, heads, LANES), F32)],
        input_output_aliases=aliases,
        compiler_params=_params(("parallel", "parallel"),
                                [((tm, k), BF16), ((k, tn), BF16), ((tm, tn), BF16), o5_block]),
        name=name,
    )(*args)


def _swiglu(a, wa_ref, wb_ref):
    ga = jnp.dot(a, wa_ref[...], preferred_element_type=F32)
    gb = jnp.dot(a, wb_ref[...], preferred_element_type=F32)
    return jax.nn.silu(ga) * gb


def _swiglu_body(a_ref, wa_ref, wb_ref, *refs, has_sample):
    n_round = (len(refs) - 1 - 2 * has_sample) // 2
    src = refs[has_sample:has_sample + n_round]
    o_ref = refs[has_sample + n_round]
    dst = refs[2 * has_sample + n_round + 1:]
    o_ref[...] = _swiglu(a_ref[...], wa_ref, wb_ref).astype(o_ref.dtype)
    _round_blocks(src, dst)
    if has_sample:
        as_ref, os_ref = refs[0], refs[has_sample + n_round + 1]

        @pl.when(pl.program_id(0) == 0)
        def _():
            os_ref[...] = _swiglu(as_ref[...], wa_ref, wb_ref).astype(os_ref.dtype)


def _matmul_swiglu(a, w13, *, tm, tn, rounding=(), sample=None, name):
    m, k = a.shape
    hidden = w13.shape[1] // 2
    assert m % tm == 0 and hidden % tn == 0
    nj = hidden // tn
    r_in, r_out, r_shape, r_blocks = _rounding_specs(rounding, (m // tm) * nj, lambda i, j: i * nj + j)
    s_in, s_out, s_shape, s_args = [], [], [], []
    if sample is not None:
        ms = sample.shape[0]
        s_in = [pl.BlockSpec((ms, k), lambda i, j: (0, 0))]
        s_out = [pl.BlockSpec((ms, tn), lambda i, j: (0, jnp.where(i == 0, j, nj - 1)))]
        s_shape = [jax.ShapeDtypeStruct((ms, hidden), BF16)]
        s_args = [sample]
        r_blocks = r_blocks + [((ms, k), BF16), ((ms, tn), BF16)]
    res = pl.pallas_call(
        functools.partial(_swiglu_body, has_sample=sample is not None),
        grid=(m // tm, nj),
        in_specs=[pl.BlockSpec((tm, k), lambda i, j: (i, 0)),
                  pl.BlockSpec((k, tn), lambda i, j: (0, j)),
                  pl.BlockSpec((k, tn), lambda i, j: (0, j + nj))] + s_in + r_in,
        out_specs=[pl.BlockSpec((tm, tn), lambda i, j: (i, j))] + s_out + r_out,
        out_shape=[jax.ShapeDtypeStruct((m, hidden), BF16)] + s_shape + r_shape,
        compiler_params=_params(("arbitrary", "arbitrary"),
                                [((tm, k), BF16), ((k, tn), BF16), ((k, tn), BF16), ((tm, tn), BF16)] + r_blocks),
        name=name,
    )(a, w13, w13, *s_args, *[w for w, _ in rounding])
    return res[0] if len(res) == 1 else tuple(res)


def _sb_prompt_body(bias_ref, q_ref, k_ref, v_ref, *refs, tq, hp, scale):
    n_round = (len(refs) - 1) // 2
    o_ref = refs[n_round]
    _round_blocks(refs[:n_round], refs[n_round + 1:])
    qi = pl.program_id(2)
    d = LANES
    row = lax.broadcasted_iota(jnp.int32, (tq, tq), 0)
    col = lax.broadcasted_iota(jnp.int32, (tq, tq), 1)
    causal = col < row
    gw = min(tq, MXU_WIDTH)
    later = (lax.broadcasted_iota(jnp.int32, (gw, gw), 0)
             > lax.broadcasted_iota(jnp.int32, (gw, gw), 1)).astype(BF16)

    def block(kb, h, acc, carry, mask):
        hs = slice(h * d, (h + 1) * d)
        ks = k_ref[pl.ds(kb * tq, tq), hs]
        vs = v_ref[pl.ds(kb * tq, tq), hs]
        z = lax.dot_general(q_ref[:, hs], ks, NT_DIMS, preferred_element_type=F32)
        z = z * scale + bias_ref[h]
        log_keep = _neg_softplus(z)
        if mask:
            log_keep = jnp.where(causal, log_keep, 0.0)
        after = _sum_over_later(log_keep, later, 2)
        w = jnp.exp(log_keep + z + after + carry)
        if mask:
            w = jnp.where(causal, w, 0.0)
        acc = acc + jnp.dot(w.astype(BF16), vs, preferred_element_type=F32)
        carry = carry + after[:, :1] + log_keep[:, :1]
        return acc, carry

    state = []
    for h in range(hp):
        state.extend(block(qi, h, jnp.zeros((tq, d), F32), jnp.zeros((tq, 1), F32), True))

    def body(it, c):
        out = []
        for h in range(hp):
            out.extend(block(qi - 1 - it, h, c[2 * h], c[2 * h + 1], False))
        return tuple(out)

    state = lax.fori_loop(0, qi, body, tuple(state))
    for h in range(hp):
        o_ref[:, h * d:(h + 1) * d] = state[2 * h].astype(o_ref.dtype)


def _sb_prompt(q, k, v, bias, *, batch, seq, heads, tq, hp, rounding=(), name):
    d = LANES
    nq = seq // tq
    ng = heads // hp
    bias_rows = jnp.broadcast_to(bias.astype(F32)[:, None, None], (heads, 1, tq))
    q_spec = pl.BlockSpec((tq, hp * d), lambda b, h, i: (b * nq + i, h))
    kv_spec = pl.BlockSpec((seq, hp * d), lambda b, h, i: (b, h))
    r_in, r_out, r_shape, r_blocks = _rounding_specs(rounding, batch * ng * nq,
                                                     lambda b, h, i: (b * ng + h) * nq + i)
    res = pl.pallas_call(
        functools.partial(_sb_prompt_body, tq=tq, hp=hp, scale=d ** -0.5),
        grid=(batch, ng, nq),
        in_specs=[pl.BlockSpec((hp, 1, tq), lambda b, h, i: (h, 0, 0)), q_spec, kv_spec, kv_spec] + r_in,
        out_specs=[q_spec] + r_out,
        out_shape=[jax.ShapeDtypeStruct((batch * seq, heads * d), BF16)] + r_shape,
        compiler_params=_params(("arbitrary", "arbitrary", "arbitrary"),
                                [((tq, hp * d), BF16)] * 2 + [((seq, hp * d), BF16)] * 2 + r_blocks),
        name=name,
    )(bias_rows, q, k, v, *[w for w, _ in rounding])
    return res[0] if not rounding else tuple(res)


def _sb_sample_body(pt_ref, q_ref, bias_ref, *refs, heads, pps, scale):
    del pt_ref
    k_refs, v_refs = refs[:pps], refs[pps:2 * pps]
    o_ref, acc_ref, carry_ref = refs[2 * pps:]
    pg = pl.program_id(1)
    page = k_refs[0].shape[0]
    d = LANES

    @pl.when(pg == 0)
    def _():
        acc_ref[...] = jnp.zeros_like(acc_ref)
        carry_ref[...] = jnp.zeros_like(carry_ref)

    row = lax.broadcasted_iota(jnp.int32, (page, page), 0)
    col = lax.broadcasted_iota(jnp.int32, (page, page), 1)
    later = (row > col).astype(BF16)
    hrow = lax.broadcasted_iota(jnp.int32, (heads, page), 0)
    hrow_d = lax.broadcasted_iota(jnp.int32, (heads, d), 0)
    q = q_ref[...]
    bias = bias_ref[...]
    acc = acc_ref[...]
    carry = carry_ref[...]
    for k_ref, v_ref in zip(k_refs, v_refs):
        kt = jnp.swapaxes(k_ref[...], 0, 1).astype(BF16)
        vt = jnp.swapaxes(v_ref[...], 0, 1).astype(BF16)
        z = jnp.zeros((heads, page), F32)
        for h in range(heads):
            zh = lax.dot_general(q, kt[h], NT_DIMS, preferred_element_type=F32)
            z = jnp.where(hrow == h, zh, z)
        z = z * scale + bias
        log_keep = _neg_softplus(z)
        after = _sum_over_later(log_keep, later, 3)
        w = jnp.exp(log_keep + z + after + carry).astype(BF16)
        carry = carry + after[:, :1] + log_keep[:, :1]
        for h in range(heads):
            oh = jnp.dot(w, vt[h], preferred_element_type=F32)
            acc = acc + jnp.where(hrow_d == h, oh, 0.0)
    acc_ref[...] = acc
    carry_ref[...] = carry

    @pl.when(pg == pl.num_programs(1) - 1)
    def _():
        o_ref[...] = acc.astype(o_ref.dtype)


def _sb_sample(q, bias, cache_k, cache_v, page_table, layer, *, pps, name):
    bsz, heads, d = q.shape
    page = cache_k.shape[2]
    n_pages = page_table.shape[1]
    assert n_pages % pps == 0
    bias_rows = jnp.broadcast_to(bias.astype(F32)[:, None], (heads, page))

    def cache_spec(r):
        return pl.BlockSpec((None, None, page, heads, d),
                            lambda b, p, pt: (pt[b, n_pages - 1 - (p * pps + r)], layer, 0, 0, 0))

    grid_spec = pltpu.PrefetchScalarGridSpec(
        num_scalar_prefetch=1,
        grid=(bsz, n_pages // pps),
        in_specs=[pl.BlockSpec((None, heads, d), lambda b, p, pt: (b, 0, 0)),
                  pl.BlockSpec((heads, page), lambda b, p, pt: (0, 0))]
                 + [cache_spec(r) for r in range(pps)] * 2,
        out_specs=pl.BlockSpec((None, heads, d), lambda b, p, pt: (b, 0, 0)),
        scratch_shapes=[pltpu.VMEM((heads, d), F32), pltpu.VMEM((heads, 1), F32)],
    )
    out = pl.pallas_call(
        functools.partial(_sb_sample_body, heads=heads, pps=pps, scale=d ** -0.5),
        grid_spec=grid_spec,
        out_shape=jax.ShapeDtypeStruct((bsz, heads, d), BF16),
        compiler_params=_params(("parallel", "arbitrary"), [((page, heads, d), F32)] * (2 * pps)),
        name=name,
    )(page_table, q.astype(BF16), bias_rows, *([cache_k] * pps), *([cache_v] * pps))
    return out.reshape(bsz, heads * d)


def _gmlp_prompt_body(u_ref, vb_ref, gn_ref, ws_ref, bst_ref, o_ref, *, groups):
    chunk = u_ref.shape[0]
    vn = _rms_rows(jax.nn.gelu(vb_ref[...]), gn_ref[...])
    row = lax.broadcasted_iota(jnp.int32, (chunk, chunk), 0)
    col = lax.broadcasted_iota(jnp.int32, (chunk, chunk), 1)
    for g in range(groups):
        sl = slice(g * LANES, (g + 1) * LANES)
        w = jnp.where(col <= row, ws_ref[g], 0.0).astype(BF16)
        s = jnp.dot(w, vn[:, sl].astype(BF16), preferred_element_type=F32) + bst_ref[:, g:g + 1]
        o_ref[:, sl] = (jax.nn.gelu(u_ref[:, sl]) * s).astype(o_ref.dtype)


def _gmlp_prompt(p, gnorm, ws, bs, *, u_col, name):
    groups, chunk, _ = ws.shape
    width = groups * LANES
    m = p.shape[0]
    return pl.pallas_call(
        functools.partial(_gmlp_prompt_body, groups=groups),
        grid=(m // chunk,),
        in_specs=[pl.BlockSpec((chunk, width), lambda i: (i, u_col)),
                  pl.BlockSpec((chunk, width), lambda i: (i, u_col + 1)),
                  pl.BlockSpec((1, width), lambda i: (0, 0)),
                  pl.BlockSpec((groups, chunk, chunk), lambda i: (0, 0, 0)),
                  pl.BlockSpec((chunk, groups), lambda i: (0, 0))],
        out_specs=pl.BlockSpec((chunk, width), lambda i: (i, 0)),
        out_shape=jax.ShapeDtypeStruct((m, width), BF16),
        compiler_params=_params(("parallel",),
                                [((chunk, width), F32)] * 2 + [((groups, chunk, chunk), F32), ((chunk, width), BF16)]),
        name=name,
    )(p, p, gnorm.reshape(1, width), ws, bs.T)


def _gmlp_sample_body(u_ref, vb_ref, gn_ref, w0_ref, b0_ref, o_ref, vn_ref):
    vn = _rms_rows(jax.nn.gelu(vb_ref[...]), gn_ref[...])
    vn_ref[...] = vn
    o_ref[...] = (jax.nn.gelu(u_ref[...]) * (w0_ref[...] * vn + b0_ref[...])).astype(o_ref.dtype)


def _gmlp_sample(p, gnorm, ws, bs, *, u_col, name):
    groups = ws.shape[0]
    width = groups * LANES
    m = p.shape[0]
    w0 = jnp.repeat(ws[:, 0, 0], LANES).reshape(1, width)
    b0 = jnp.repeat(bs[:, 0], LANES).reshape(1, width)
    row = lambda c: pl.BlockSpec((m, width), lambda i: (0, c))
    vec = pl.BlockSpec((1, width), lambda i: (0, 0))
    return pl.pallas_call(
        _gmlp_sample_body,
        grid=(1,),
        in_specs=[row(u_col), row(u_col + 1), vec, vec, vec],
        out_specs=[row(0), row(0)],
        out_shape=[jax.ShapeDtypeStruct((m, width), BF16), jax.ShapeDtypeStruct((m, width), F32)],
        compiler_params=_params(("arbitrary",), [((m, width), F32)] * 4),
        name=name,
    )(p, p, gnorm.reshape(1, width), w0, b0)


HGRN_LEVELS = 7


def _hgrn_gates(fpre, lb):
    t = jnp.log(1.0 + jnp.exp(-jnp.abs(fpre)))
    log_sig = jnp.minimum(fpre, 0.0) - t
    log_sig_neg = jnp.minimum(-fpre, 0.0) - t
    a = jnp.log(lb)
    c = jnp.log1p(-lb) + log_sig
    logf = jnp.maximum(a, c) + jnp.log(1.0 + jnp.exp(-jnp.abs(a - c)))
    return logf, (1.0 - lb) * jnp.exp(log_sig_neg)


def _hgrn_finish(o, g, gn):
    return _rms_rows(o, gn) * jax.nn.silu(g)


def _hgrn_tables():
    n = LANES
    t = np.arange(n)[:, None]
    r = np.arange(n)[None, :]
    blocks = []
    for level in range(HGRN_LEVELS):
        c = 1 << level
        mid = (t // (2 * c)) * (2 * c) + c
        second = (t % (2 * c)) >= c
        blocks.append(np.where(second, (r >= mid) & (r <= t), (r > t) & (r < mid)))
    blocks += [r <= t, r > t, np.ones((SUBLANES, n), bool)]
    ranges = np.concatenate(blocks, axis=0).astype(np.float32)
    diff = t ^ r
    level_of = np.where(t == r, -1, np.where(r < t, np.floor(np.log2(np.maximum(diff, 1))), HGRN_LEVELS))
    return (jnp.asarray(np.concatenate([ranges, ranges], axis=1), BF16),
            jnp.asarray(level_of.astype(np.int32)))


def _hgrn_prompt_body(q_ref, f_ref, i_ref, g_ref, lb_ref, gn_ref, rng_ref, lvl_ref, *refs, tb):
    n_round = (len(refs) - 3) // 2
    o_ref, s_ref = refs[n_round:n_round + 2]
    st_ref = refs[-1]
    _round_blocks(refs[:n_round], refs[n_round + 2:-1])
    step = pl.program_id(2)
    n = LANES

    @pl.when(step == 0)
    def _():
        st_ref[...] = jnp.zeros_like(st_ref)

    lb = lb_ref[...]
    gn = gn_ref[...]
    level_of = lvl_ref[...]
    st = st_ref[...]
    tiles = [slice(i * n, (i + 1) * n) for i in range(tb // n)]
    logf_all, k_all = _hgrn_gates(f_ref[...], lb)
    rhs = jnp.concatenate([jnp.concatenate([piece[rs] for rs in tiles], axis=1)
                           for piece in _bf16_pieces(logf_all, 2)], axis=0)
    decay_all = jnp.exp(jnp.dot(rng_ref[...], rhs, preferred_element_type=F32))
    for rs in tiles:
        q = q_ref[rs, :]
        v = i_ref[rs, :]
        k = k_all[rs]
        decay = decay_all[:, rs]
        qb, kb, vb = q.astype(BF16), k.astype(BF16), v.astype(BF16)
        scores = jnp.zeros((n, n), F32)
        for level in range(HGRN_LEVELS):
            dl = decay[level * n:(level + 1) * n].astype(BF16)
            pair = lax.dot_general(qb * dl, kb * dl, NT_DIMS, preferred_element_type=F32)
            scores = jnp.where(level_of == level, pair, scores)
        same = lax.dot_general(qb, kb, NT_DIMS, preferred_element_type=F32)
        scores = jnp.where(level_of == -1, same, scores)
        from_start = decay[HGRN_LEVELS * n:(HGRN_LEVELS + 1) * n].astype(BF16)
        to_end = decay[(HGRN_LEVELS + 1) * n:(HGRN_LEVELS + 2) * n].astype(BF16)
        total = decay[(HGRN_LEVELS + 2) * n:(HGRN_LEVELS + 2) * n + 1]
        o = (jnp.dot(scores.astype(BF16), vb, preferred_element_type=F32)
             + lax.dot_general(qb * from_start, st.astype(BF16), NT_DIMS, preferred_element_type=F32))
        st = st * total + lax.dot_general(vb, kb * to_end, TN_DIMS, preferred_element_type=F32)
        o_ref[rs, :] = _hgrn_finish(o, g_ref[rs, :], gn).astype(o_ref.dtype)
    st_ref[...] = st

    @pl.when(step == pl.num_programs(2) - 1)
    def _():
        s_ref[...] = st.T


def _hgrn_prompt(p, lb, gnorm, *, batch, seq, heads, tb, rounding=(), name):
    d = LANES
    nt = seq // tb
    ranges, level_of = _hgrn_tables()
    col = lambda c: pl.BlockSpec((tb, d), lambda b, h, t: (b * nt + t, c * heads + h))
    r_in, r_out, r_shape, r_blocks = _rounding_specs(rounding, batch * heads * nt,
                                                     lambda b, h, t: (b * heads + h) * nt + t)
    res = pl.pallas_call(
        functools.partial(_hgrn_prompt_body, tb=tb),
        grid=(batch, heads, nt),
        in_specs=[col(0), col(1), col(2), col(3),
                  pl.BlockSpec((None, 1, d), lambda b, h, t: (h, 0, 0)),
                  pl.BlockSpec((1, d), lambda b, h, t: (0, 0)),
                  pl.BlockSpec(ranges.shape, lambda b, h, t: (0, 0)),
                  pl.BlockSpec(level_of.shape, lambda b, h, t: (0, 0))] + r_in,
        out_specs=[pl.BlockSpec((tb, d), lambda b, h, t: (b * nt + t, h)),
                   pl.BlockSpec((None, None, d, d), lambda b, h, t: (b, h, 0, 0))] + r_out,
        out_shape=[jax.ShapeDtypeStruct((batch * seq, heads * d), BF16),
                   jax.ShapeDtypeStruct((batch, heads, d, d), F32)] + r_shape,
        scratch_shapes=[pltpu.VMEM((d, d), F32)],
        compiler_params=_params(("arbitrary", "arbitrary", "arbitrary"),
                                [((tb, d), F32)] * 5 + [((d, d), F32), (ranges.shape, BF16)] + r_blocks),
        name=name,
    )(p, p, p, p, lb.reshape(heads, 1, d), gnorm.reshape(1, d), ranges, level_of, *[w for w, _ in rounding])
    return tuple(res)


def _hgrn_sample_body(q_ref, f_ref, i_ref, g_ref, lb_ref, gn_ref, s_ref, o_ref, so_ref, oacc_ref, *, heads):
    d = LANES
    q = q_ref[...]
    v = i_ref[...]
    logf, k = _hgrn_gates(f_ref[...], lb_ref[...])
    f = jnp.exp(logf)
    eye = lax.broadcasted_iota(jnp.int32, (d, d), 0) == lax.broadcasted_iota(jnp.int32, (d, d), 1)

    def to_col(r):
        return jnp.sum(jnp.where(eye, r, 0.0), axis=1, keepdims=True)

    for h in range(heads):
        hs = slice(h, h + 1)
        s_new = to_col(f[hs]) * s_ref[h] + to_col(k[hs]) * v[hs]
        so_ref[h] = s_new
        oacc_ref[hs, :] = jnp.sum(to_col(q[hs]) * s_new, axis=0, keepdims=True)
    o_ref[...] = _hgrn_finish(oacc_ref[...], g_ref[...], gn_ref[...]).astype(o_ref.dtype)


def _hgrn_sample(p, lb, gnorm, state, layer, *, heads, name):
    bsz = p.shape[0]
    d = LANES
    p4 = p.reshape(bsz, 4, heads, d)
    part = lambda c: pl.BlockSpec((None, None, heads, d), lambda b: (b, c, 0, 0))
    o, s_new = pl.pallas_call(
        functools.partial(_hgrn_sample_body, heads=heads),
        grid=(bsz,),
        in_specs=[part(0), part(1), part(2), part(3),
                  pl.BlockSpec((heads, d), lambda b: (0, 0)),
                  pl.BlockSpec((1, d), lambda b: (0, 0)),
                  pl.BlockSpec((None, None, heads, d, d), lambda b: (layer, b, 0, 0, 0))],
        out_specs=[pl.BlockSpec((None, heads, d), lambda b: (b, 0, 0)),
                   pl.BlockSpec((None, heads, d, d), lambda b: (b, 0, 0, 0))],
        out_shape=[jax.ShapeDtypeStruct((bsz, heads, d), BF16),
                   jax.ShapeDtypeStruct((bsz, heads, d, d), F32)],
        scratch_shapes=[pltpu.VMEM((heads, d), F32)],
        compiler_params=_params(("parallel",), [((heads, d, d), F32)] * 2),
        name=name,
    )(p4, p4, p4, p4, lb.reshape(heads, d), gnorm.reshape(1, d), state)
    return o.reshape(bsz, heads * d), s_new


def _attend(q, k_ref, v_ref, heads, scale):
    outs = []
    for h in range(heads):
        sl = slice(h * LANES, (h + 1) * LANES)
        s = lax.dot_general(q[:, sl].astype(BF16), k_ref[:, sl].astype(BF16), NT_DIMS,
                            preferred_element_type=F32) * scale
        e = jnp.exp(s - jnp.max(s, axis=-1, keepdims=True))
        prob = e * (1.0 / jnp.sum(e, axis=-1, keepdims=True))
        outs.append(jnp.dot(prob.astype(BF16), v_ref[:, sl].astype(BF16), preferred_element_type=F32))
    return jnp.concatenate(outs, axis=1)


def _xattn_prompt_body(x_ref, gx_ref, gf_ref, wq_ref, wo_ref, k_ref, v_ref, y_ref, hf_ref, *, heads, scale):
    x = x_ref[...]
    h = _rms_rows(x, gx_ref[...]).astype(BF16)
    q = jnp.dot(h, wq_ref[...], preferred_element_type=F32)
    att = _attend(q, k_ref, v_ref, heads, scale).astype(BF16)
    y = x + jnp.dot(att, wo_ref[...], preferred_element_type=F32)
    y_ref[...] = y
    hf_ref[...] = _rms_rows(y, gf_ref[...]).astype(hf_ref.dtype)


def _xattn_prompt(x, gx, gf, wq, wo, layer, mem_k, mem_v, *, batch, seq, tq, name):
    d = x.shape[1]
    xw = wq.shape[2]
    heads = xw // LANES
    mlen = mem_k.shape[1]
    nt = seq // tq
    rows = pl.BlockSpec((tq, d), lambda b, t: (b * nt + t, 0))
    vec = pl.BlockSpec((1, d), lambda b, t: (0, 0))
    mem = pl.BlockSpec((None, mlen, xw), lambda b, t: (b, 0, 0))
    return pl.pallas_call(
        functools.partial(_xattn_prompt_body, heads=heads, scale=LANES ** -0.5),
        grid=(batch, nt),
        in_specs=[rows, vec, vec,
                  pl.BlockSpec((None, d, xw), lambda b, t: (layer, 0, 0)),
                  pl.BlockSpec((None, xw, d), lambda b, t: (layer, 0, 0)),
                  mem, mem],
        out_specs=[rows, rows],
        out_shape=[jax.ShapeDtypeStruct(x.shape, F32), jax.ShapeDtypeStruct(x.shape, BF16)],
        compiler_params=_params(("parallel", "parallel"),
                                [((tq, d), F32)] * 2 + [((tq, d), BF16), ((d, xw), BF16), ((xw, d), BF16),
                                                        ((mlen, xw), F32), ((mlen, xw), F32)]),
        name=name,
    )(x, gx.reshape(1, d), gf.reshape(1, d), wq, wo, mem_k, mem_v)


def _xattn_sample_body(x_ref, gx_ref, gf_ref, wq_ref, wo_ref, k_ref, v_ref, y_ref, hf_ref, q_sc, att_sc,
                       *, heads, scale):
    b = pl.program_id(0)

    @pl.when(b == 0)
    def _():
        h = _rms_rows(x_ref[...], gx_ref[...]).astype(BF16)
        q_sc[...] = jnp.dot(h, wq_ref[...], preferred_element_type=F32)
        att_sc[...] = jnp.zeros_like(att_sc)

    att = _attend(q_sc[...], k_ref, v_ref, heads, scale)
    mine = lax.broadcasted_iota(jnp.int32, att.shape, 0) == b
    att_sc[...] += jnp.where(mine, att, 0.0)

    @pl.when(b == pl.num_programs(0) - 1)
    def _():
        y = x_ref[...] + jnp.dot(att_sc[...].astype(BF16), wo_ref[...], preferred_element_type=F32)
        y_ref[...] = y
        hf_ref[...] = _rms_rows(y, gf_ref[...]).astype(hf_ref.dtype)


def _xattn_sample(x, gx, gf, wq, wo, mem_k, mem_v, layer, name):
    bsz, d = x.shape
    xw = wq.shape[2]
    heads = xw // LANES
    mlen = mem_k.shape[2]
    mk = mem_k.reshape(mem_k.shape[0], bsz, mlen, xw)
    mv = mem_v.reshape(mem_v.shape[0], bsz, mlen, xw)
    rows = pl.BlockSpec((bsz, d), lambda b: (0, 0))
    vec = pl.BlockSpec((1, d), lambda b: (0, 0))
    mem = pl.BlockSpec((None, None, mlen, xw), lambda b: (layer, b, 0, 0))
    return pl.pallas_call(
        functools.partial(_xattn_sample_body, heads=heads, scale=LANES ** -0.5),
        grid=(bsz,),
        in_specs=[rows, vec, vec,
                  pl.BlockSpec((None, d, xw), lambda b: (layer, 0, 0)),
                  pl.BlockSpec((None, xw, d), lambda b: (layer, 0, 0)),
                  mem, mem],
        out_specs=[rows, rows],
        out_shape=[jax.ShapeDtypeStruct(x.shape, F32), jax.ShapeDtypeStruct(x.shape, BF16)],
        scratch_shapes=[pltpu.VMEM((bsz, xw), F32), pltpu.VMEM((bsz, xw), F32)],
        compiler_params=_params(("arbitrary",),
                                [((d, xw), BF16), ((xw, d), BF16), ((mlen, xw), F32), ((mlen, xw), F32)]),
        name=name,
    )(x, gx.reshape(1, d), gf.reshape(1, d), wq, wo, mk, mv)


def kernel(x_prompt, x_sample, mem_prompt, cache_sb_k, cache_sb_v, cache_mem_k, cache_mem_v, state_hgrn, page_table, norm_mix, norm_mem, norm_xattn, norm_ffn, norm_final, w_in_ab, w_out_ab, sb_bias, gmlp_norm, gmlp_ws, gmlp_bs, w_in_c, w_out_c, hgrn_lb, hgrn_gnorm, xattn_wq, xattn_wkv, xattn_wo, ffn_w13, ffn_w2):
    bp, tp, d = x_prompt.shape
    bd, ts, _ = x_sample.shape
    assert ts == 1, "the sample group carries one new token per sequence"
    depth = norm_mix.shape[0]
    n_even = w_in_ab.shape[0]
    a_heads = sb_bias.shape[1]
    a_width = a_heads * LANES
    groups = gmlp_ws.shape[1]
    b_width = groups * LANES
    c_heads = hgrn_lb.shape[1] // LANES
    mem_len = mem_prompt.shape[1]
    xw = xattn_wq.shape[2]
    assert a_width == b_width and w_in_ab.shape[2] == 3 * a_width + 2 * b_width

    xp = x_prompt.reshape(bp * tp, d)
    xs = x_sample.reshape(bd, d)
    mem = mem_prompt.reshape(bp * mem_len, d)

    lb_all = jnp.cumsum(jax.nn.softmax(hgrn_lb.astype(F32), axis=0), axis=0)
    lb_all = lb_all - lb_all[:1]

    tm_p, tm_s = 1024, bd
    ffn_tn = 256

    w_in = w_in_ab[0].astype(BF16)
    wq_all = xattn_wq.astype(BF16)
    wo_all = xattn_wo.astype(BF16)
    wkv_all = xattn_wkv.astype(BF16)

    sbk_p, sbv_p = None, None
    sbk_s, sbv_s, gv_s, hs_p, hs_s, mk_p, mv_p = [], [], [], [], [], [], []
    for l in range(depth):
        j = l // 2
        hp = _rmsnorm(xp, norm_mix[l], BF16, "norm_mix_p")
        hs = _rmsnorm(xs, norm_mix[l], BF16, "norm_mix_s")
        if l % 2 == 0:
            qp = _matmul([hp], w_in, w_cols=(0, a_width), out_dtype=BF16, tm=tm_p, tn=1024, name="in_q_p")
            kp2, sbk_p = _matmul_heads(hp, w_in, layer=None, w_cols=(a_width, a_width), batch=bp, seq=tp, slot=j,
                                       n_slots=n_even, stacked=sbk_p, tm=tm_p, name="in_k_p")
            vp2, sbv_p = _matmul_heads(hp, w_in, layer=None, w_cols=(2 * a_width, a_width), batch=bp, seq=tp,
                                       slot=j, n_slots=n_even, stacked=sbv_p, tm=tm_p, name="in_v_p")
            uv = _matmul([hp], w_in, w_cols=(3 * a_width, 2 * b_width), out_dtype=F32, tm=tm_p, tn=1024,
                         name="in_uv_p")
            oa, w13, w_out = _sb_prompt(qp, kp2, vp2, sb_bias[j], batch=bp, seq=tp, heads=a_heads, tq=512, hp=2,
                                        rounding=[(ffn_w13, l), (w_out_ab, j)], name="sb_p")
            ob = _gmlp_prompt(uv, gmlp_norm[j], gmlp_ws[j], gmlp_bs[j], u_col=0, name="gmlp_p")
            ps = _matmul([hs], w_in, out_dtype=F32, tm=tm_s, tn=1024, name="in_ab_s")
            oa_s = _sb_sample(ps[:, :a_width].reshape(bd, a_heads, LANES), sb_bias[j], cache_sb_k, cache_sb_v,
                              page_table, j, pps=8, name="sb_s")
            ob_s, vn = _gmlp_sample(ps, gmlp_norm[j], gmlp_ws[j], gmlp_bs[j], u_col=3 * a_width // b_width,
                                    name="gmlp_s")
            xp, xs = _matmul([oa, ob], w_out, out_dtype=F32, tm=tm_p, tn=1024, resid=xp,
                             sample=([oa_s, ob_s], xs), name="out_ab")
            sbk_s.append(ps[:, a_width:2 * a_width].reshape(bd, ts, a_heads, LANES))
            sbv_s.append(ps[:, 2 * a_width:3 * a_width].reshape(bd, ts, a_heads, LANES))
            gv_s.append(vn.reshape(bd, ts, groups, LANES))
        else:
            pp, ps = _matmul([hp], w_in, out_dtype=F32, tm=tm_p, tn=1024, sample=([hs], None), name="in_c")
            oc, s_p, w13, w_out = _hgrn_prompt(pp, lb_all[j], hgrn_gnorm[j], batch=bp, seq=tp, heads=c_heads,
                                               tb=1024, rounding=[(ffn_w13, l), (w_out_c, j)], name="hgrn_p")
            oc_s, s_s = _hgrn_sample(ps, lb_all[j], hgrn_gnorm[j], state_hgrn, j, heads=c_heads, name="hgrn_s")
            xp, xs = _matmul([oc], w_out, out_dtype=F32, tm=tm_p, tn=1024, resid=xp, sample=([oc_s], xs),
                             name="out_c")
            hs_p.append(s_p)
            hs_s.append(s_s)
        hm = _rmsnorm(mem, norm_mem[l], BF16, "norm_mem")
        kv = _matmul([hm], wkv_all, layer=l, out_dtype=F32, tm=bp * mem_len, tn=512, name="mem_kv")
        kp = kv[:, :xw].reshape(bp, mem_len, xw)
        vp = kv[:, xw:].reshape(bp, mem_len, xw)
        mk_p.append(kp.reshape(bp, mem_len, xw // LANES, LANES))
        mv_p.append(vp.reshape(bp, mem_len, xw // LANES, LANES))
        xp, hfp = _xattn_prompt(xp, norm_xattn[l], norm_ffn[l], wq_all, wo_all, l, kp, vp, batch=bp, seq=tp, tq=256,
                                name="xattn_p")
        xs, hfs = _xattn_sample(xs, norm_xattn[l], norm_ffn[l], wq_all, wo_all, cache_mem_k, cache_mem_v, l,
                                "xattn_s")
        nxt = [] if l + 1 == depth else [(w_in_c, (l + 1) // 2)] if l % 2 == 0 else [(w_in_ab, (l + 1) // 2)]
        mid, mid_s, w2, *w_next = _matmul_swiglu(hfp, w13, tm=2 * tm_p, tn=ffn_tn, sample=hfs,
                                                 rounding=[(ffn_w2, l)] + nxt, name="ffn13")
        if w_next:
            w_in = w_next[0]
        xp, xs = _matmul([mid], w2, out_dtype=F32, tm=512, tn=1024, resid=xp, w_single=True,
                         sample=([mid_s], xs), name="ffn2")
    y_prompt = _rmsnorm(xp, norm_final, F32, "norm_final_p").reshape(bp, tp, d)
    y_sample = _rmsnorm(xs, norm_final, F32, "norm_final_s").reshape(bd, ts, d)
    return (y_prompt, y_sample, sbk_p, sbv_p,
            jnp.stack(sbk_s, axis=1), jnp.stack(sbv_s, axis=1),
            jnp.stack(gv_s, axis=1),
            jnp.stack(hs_p, axis=0), jnp.stack(hs_s, axis=0),
            jnp.stack(mk_p, axis=0), jnp.stack(mv_p, axis=0))
```

```python
import functools

import numpy as np

import jax
import jax.numpy as jnp
from jax import lax
from jax.experimental import pallas as pl
from jax.experimental.pallas import tpu as pltpu

F32 = jnp.float32
BF16 = jnp.bfloat16
EPS = 1e-6
LANES = 128
SUBLANES = 8
MXU_WIDTH = 256
VMEM_CAP = 60 * 1024 * 1024
VMEM_SLACK = 12 * 1024 * 1024
NT_DIMS = (((1,), (1,)), ((), ()))
TN_DIMS = (((0,), (0,)), ((), ()))


def _nbytes(shape, dtype):
    n = 1
    for s in shape:
        n *= s
    return n * jnp.dtype(dtype).itemsize


def _params(semantics, blocks):
    need = sum(_nbytes(b[0], b[1]) * (b[2] if len(b) > 2 else 2) for b in blocks) + VMEM_SLACK
    return pltpu.CompilerParams(dimension_semantics=semantics,
                                vmem_limit_bytes=int(min(need, VMEM_CAP)))


def _rms_rows(x, g):
    r = lax.rsqrt(jnp.mean(x * x, axis=-1, keepdims=True) + EPS)
    return (x * r) * g


def _bf16_pieces(x, terms):
    out = []
    for _ in range(terms):
        piece = x.astype(BF16)
        out.append(piece)
        x = x - piece.astype(F32)
    return out


def _neg_softplus(z):
    return -(jnp.maximum(z, 0.0) + jnp.log(1.0 + jnp.exp(-jnp.abs(z))))


def _sum_over_later(log_keep, later, terms):
    w = later.shape[0]
    groups = [log_keep[:, g * w:(g + 1) * w] for g in range(log_keep.shape[1] // w)]
    inside = []
    for lk in groups:
        acc = None
        for piece in _bf16_pieces(lk, terms):
            part = jnp.dot(piece, later, preferred_element_type=F32)
            acc = part if acc is None else acc + part
        inside.append(acc)
    out, beyond = [], None
    for lk, acc in zip(reversed(groups), reversed(inside)):
        out.append(acc if beyond is None else acc + beyond)
        total = acc[:, :1] + lk[:, :1]
        beyond = total if beyond is None else beyond + total
    return jnp.concatenate(out[::-1], axis=1) if len(out) > 1 else out[0]


def _norm_body(x_ref, g_ref, o_ref):
    o_ref[...] = _rms_rows(x_ref[...], g_ref[...]).astype(o_ref.dtype)


def _rmsnorm(x, g, out_dtype, name):
    m, d = x.shape
    tm = min(m, 256)
    return pl.pallas_call(
        _norm_body,
        grid=(m // tm,),
        in_specs=[pl.BlockSpec((tm, d), lambda i: (i, 0)),
                  pl.BlockSpec((1, d), lambda i: (0, 0))],
        out_specs=pl.BlockSpec((tm, d), lambda i: (i, 0)),
        out_shape=jax.ShapeDtypeStruct((m, d), out_dtype),
        compiler_params=_params(("parallel",), [((tm, d), F32), ((tm, d), out_dtype)]),
        name=name,
    )(x, g.reshape(1, d))


def _w_spec(w, layer, k, tn, col_of, **mode):
    if w.ndim == 3:
        return pl.BlockSpec((None, k, tn), lambda j, i: (layer, 0, col_of(j)), **mode)
    return pl.BlockSpec((k, tn), lambda j, i: (0, col_of(j)), **mode)


BF16_ROWS = 16


def _rounding_specs(jobs, n_steps, step_of):
    in_specs, out_specs, out_shape, blocks = [], [], [], []
    for w, layer in jobs:
        rows, cols = w.shape[-2:]
        r = BF16_ROWS
        while rows % r or rows // r > n_steps:
            r += BF16_ROWS
        last = rows // r - 1
        in_specs.append(pl.BlockSpec((None, r, cols),
                                     lambda *g, layer=layer, last=last: (layer, jnp.minimum(step_of(*g), last), 0)))
        out_specs.append(pl.BlockSpec((r, cols), lambda *g, last=last: (jnp.minimum(step_of(*g), last), 0)))
        out_shape.append(jax.ShapeDtypeStruct((rows, cols), BF16))
        blocks += [((r, cols), F32), ((r, cols), BF16)]
    return in_specs, out_specs, out_shape, blocks


def _round_blocks(src_refs, dst_refs):
    for src, dst in zip(src_refs, dst_refs):
        dst[...] = src[...].astype(BF16)


def _project(a_refs, w_ref, k_sizes):
    acc, off = None, 0
    for a_ref, kk in zip(a_refs, k_sizes):
        part = jnp.dot(a_ref[...], w_ref[off:off + kk, :], preferred_element_type=F32)
        acc = part if acc is None else acc + part
        off += kk
    return acc


def _mm_body(*refs, k_sizes, has_resid, has_sample, out_scale):
    n_a = len(k_sizes)
    a_refs, w_ref = refs[:n_a], refs[n_a]
    pos = n_a + 1
    r_ref = refs[pos] if has_resid else None
    pos += has_resid
    acc = _project(a_refs, w_ref, k_sizes)
    if out_scale is not None:
        acc = acc * out_scale
    if has_resid:
        acc = acc + r_ref[...]
    o_ref = refs[-2] if has_sample else refs[-1]
    o_ref[...] = acc.astype(o_ref.dtype)
    if has_sample:
        s_refs = refs[pos:pos + n_a]
        rs_ref = refs[pos + n_a] if has_resid else None
        os_ref = refs[-1]

        @pl.when(pl.program_id(1) == 0)
        def _():
            acc_s = _project(s_refs, w_ref, k_sizes)
            if has_resid:
                acc_s = acc_s + rs_ref[...]
            os_ref[...] = acc_s.astype(os_ref.dtype)


def _matmul(a_list, w, *, out_dtype, tm, tn, layer=None, resid=None, w_cols=None, w_single=False,
            sample=None, out_scale=None, name):
    m = a_list[0].shape[0]
    k_sizes = tuple(a.shape[1] for a in a_list)
    k = w.shape[-2]
    first, n = w_cols if w_cols is not None else (0, w.shape[-1])
    assert sum(k_sizes) == k and m % tm == 0 and n % tn == 0 and first % tn == 0
    assert out_scale is None or (resid is None and sample is None)
    j0 = first // tn
    in_specs = [pl.BlockSpec((tm, kk), lambda j, i: (i, 0)) for kk in k_sizes]
    w_mode = dict(pipeline_mode=pl.Buffered(1)) if w_single else {}
    in_specs.append(_w_spec(w, layer, k, tn, lambda j: j0 + j, **w_mode))
    blocks = [((tm, kk), BF16) for kk in k_sizes] + [((k, tn), w.dtype, 1 if w_single else 2), ((tm, tn), out_dtype)]
    args = list(a_list) + [w]
    if resid is not None:
        in_specs.append(pl.BlockSpec((tm, tn), lambda j, i: (i, j)))
        blocks.append(((tm, tn), F32))
        args.append(resid)
    out_specs = [pl.BlockSpec((tm, tn), lambda j, i: (i, j))]
    out_shape = [jax.ShapeDtypeStruct((m, n), out_dtype)]
    if sample is not None:
        a_s_list, resid_s = sample
        ms = a_s_list[0].shape[0]
        assert (resid_s is None) == (resid is None) and tuple(a.shape[1] for a in a_s_list) == k_sizes
        in_specs += [pl.BlockSpec((ms, kk), lambda j, i: (0, 0)) for kk in k_sizes]
        args += list(a_s_list)
        if resid_s is not None:
            in_specs.append(pl.BlockSpec((ms, tn), lambda j, i: (0, j)))
            args.append(resid_s)
        out_specs.append(pl.BlockSpec((ms, tn), lambda j, i: (0, j)))
        out_shape.append(jax.ShapeDtypeStruct((ms, n), out_dtype))
        blocks += [((ms, k), BF16), ((ms, tn), F32), ((ms, tn), F32)]
    res = pl.pallas_call(
        functools.partial(_mm_body, k_sizes=k_sizes, has_resid=resid is not None, has_sample=sample is not None,
                          out_scale=out_scale),
        grid=(n // tn, m // tm),
        in_specs=in_specs,
        out_specs=out_specs,
        out_shape=out_shape,
        compiler_params=_params(("parallel", "arbitrary"), blocks),
        name=name,
    )(*args)
    return res[0] if sample is None else tuple(res)


def _mm_heads_body(a_ref, w_ref, *rest, hb, slot):
    o2_ref, o5_ref = rest[-2], rest[-1]
    acc = jnp.dot(a_ref[...], w_ref[...], preferred_element_type=F32)
    o2_ref[...] = acc.astype(o2_ref.dtype)
    heads = acc.reshape(acc.shape[0], hb, LANES)
    if len(o5_ref.shape) == 3:
        o5_ref[...] = heads
    else:
        for s in range(o5_ref.shape[0]):
            o5_ref[s] = heads if s == slot else jnp.zeros_like(heads)


def _matmul_heads(a, w, *, layer, w_cols, batch, seq, slot, n_slots, stacked, tm, name):
    m, k = a.shape
    first, n = w_cols
    heads = n // LANES
    hb = SUBLANES
    tn = hb * LANES
    tps = seq // tm
    assert m == batch * seq and seq % tm == 0 and n % tn == 0 and first % tn == 0
    j0 = first // tn
    in_specs = [pl.BlockSpec((tm, k), lambda j, i: (i, 0)),
                _w_spec(w, layer, k, tn, lambda j: j0 + j)]
    args = [a, w]
    if stacked is None:
        aliases = {}
        o5_spec = pl.BlockSpec((None, n_slots, tm, hb, LANES), lambda j, i: (i // tps, 0, i % tps, j, 0))
        o5_block = ((n_slots, tm, tn), F32)
    else:
        in_specs.append(pl.BlockSpec(memory_space=pl.ANY))
        args.append(stacked)
        aliases = {2: 1}
        o5_spec = pl.BlockSpec((None, None, tm, hb, LANES), lambda j, i: (i // tps, slot, i % tps, j, 0))
        o5_block = ((tm, tn), F32)
    return pl.pallas_call(
        functools.partial(_mm_heads_body, hb=hb, slot=slot),
        grid=(n // tn, m // tm),
        in_specs=in_specs,
        out_specs=[pl.BlockSpec((tm, tn), lambda j, i: (i, j)), o5_spec],
        out_shape=[jax.ShapeDtypeStruct((m, n), BF16),
                   jax.ShapeDtypeStruct((batch, n_slots, seq, heads, LANES), F32)],
        input_output_aliases=aliases,
        compiler_params=_params(("parallel", "parallel"),
                                [((tm, k), BF16), ((k, tn), BF16), ((tm, tn), BF16), o5_block]),
        name=name,
    )(*args)


def _swiglu(a, wa_ref, wb_ref):
    ga = jnp.dot(a, wa_ref[...], preferred_element_type=F32)
    gb = jnp.dot(a, wb_ref[...], preferred_element_type=F32)
    return jax.nn.silu(ga) * gb


def _swiglu_body(a_ref, wa_ref, wb_ref, *refs, has_sample):
    n_round = (len(refs) - 1 - 2 * has_sample) // 2
    src = refs[has_sample:has_sample + n_round]
    o_ref = refs[has_sample + n_round]
    dst = refs[2 * has_sample + n_round + 1:]
    o_ref[...] = _swiglu(a_ref[...], wa_ref, wb_ref).astype(o_ref.dtype)
    _round_blocks(src, dst)
    if has_sample:
        as_ref, os_ref = refs[0], refs[has_sample + n_round + 1]

        @pl.when(pl.program_id(0) == 0)
        def _():
            os_ref[...] = _swiglu(as_ref[...], wa_ref, wb_ref).astype(os_ref.dtype)


def _matmul_swiglu(a, w13, *, tm, tn, rounding=(), sample=None, name):
    m, k = a.shape
    hidden = w13.shape[1] // 2
    assert m % tm == 0 and hidden % tn == 0
    nj = hidden // tn
    r_in, r_out, r_shape, r_blocks = _rounding_specs(rounding, (m // tm) * nj, lambda i, j: i * nj + j)
    s_in, s_out, s_shape, s_args = [], [], [], []
    if sample is not None:
        ms = sample.shape[0]
        s_in = [pl.BlockSpec((ms, k), lambda i, j: (0, 0))]
        s_out = [pl.BlockSpec((ms, tn), lambda i, j: (0, jnp.where(i == 0, j, nj - 1)))]
        s_shape = [jax.ShapeDtypeStruct((ms, hidden), BF16)]
        s_args = [sample]
        r_blocks = r_blocks + [((ms, k), BF16), ((ms, tn), BF16)]
    res = pl.pallas_call(
        functools.partial(_swiglu_body, has_sample=sample is not None),
        grid=(m // tm, nj),
        in_specs=[pl.BlockSpec((tm, k), lambda i, j: (i, 0)),
                  pl.BlockSpec((k, tn), lambda i, j: (0, j)),
                  pl.BlockSpec((k, tn), lambda i, j: (0, j + nj))] + s_in + r_in,
        out_specs=[pl.BlockSpec((tm, tn), lambda i, j: (i, j))] + s_out + r_out,
        out_shape=[jax.ShapeDtypeStruct((m, hidden), BF16)] + s_shape + r_shape,
        compiler_params=_params(("arbitrary", "arbitrary"),
                                [((tm, k), BF16), ((k, tn), BF16), ((k, tn), BF16), ((tm, tn), BF16)] + r_blocks),
        name=name,
    )(a, w13, w13, *s_args, *[w for w, _ in rounding])
    return res[0] if len(res) == 1 else tuple(res)


def _sb_prompt_body(bias_ref, q_ref, k_ref, v_ref, *refs, tq, hp):
    n_round = (len(refs) - 1) // 2
    o_ref = refs[n_round]
    _round_blocks(refs[:n_round], refs[n_round + 1:])
    qi = pl.program_id(2)
    d = LANES
    row = lax.broadcasted_iota(jnp.int32, (tq, tq), 0)
    col = lax.broadcasted_iota(jnp.int32, (tq, tq), 1)
    causal = col < row
    gw = min(tq, MXU_WIDTH)
    later = (lax.broadcasted_iota(jnp.int32, (gw, gw), 0)
             > lax.broadcasted_iota(jnp.int32, (gw, gw), 1)).astype(BF16)

    def block(kb, h, acc, carry, mask):
        hs = slice(h * d, (h + 1) * d)
        ks = k_ref[pl.ds(kb * tq, tq), hs]
        vs = v_ref[pl.ds(kb * tq, tq), hs]
        z = lax.dot_general(q_ref[:, hs], ks, NT_DIMS, preferred_element_type=F32)
        z = z + bias_ref[h]
        log_keep = _neg_softplus(z)
        if mask:
            log_keep = jnp.where(causal, log_keep, 0.0)
        after = _sum_over_later(log_keep, later, 2)
        w = jnp.exp(log_keep + z + after + carry)
        if mask:
            w = jnp.where(causal, w, 0.0)
        acc = acc + jnp.dot(w.astype(BF16), vs, preferred_element_type=F32)
        carry = carry + after[:, :1] + log_keep[:, :1]
        return acc, carry

    state = []
    for h in range(hp):
        state.extend(block(qi, h, jnp.zeros((tq, d), F32), jnp.zeros((tq, 1), F32), True))

    def body(it, c):
        out = []
        for h in range(hp):
            out.extend(block(qi - 1 - it, h, c[2 * h], c[2 * h + 1], False))
        return tuple(out)

    state = lax.fori_loop(0, qi, body, tuple(state))
    for h in range(hp):
        o_ref[:, h * d:(h + 1) * d] = state[2 * h].astype(o_ref.dtype)


def _sb_prompt(q, k, v, bias, *, batch, seq, heads, tq, hp, rounding=(), name):
    d = LANES
    nq = seq // tq
    ng = heads // hp
    bias_rows = jnp.broadcast_to(bias.astype(F32)[:, None, None], (heads, 1, tq))
    q_spec = pl.BlockSpec((tq, hp * d), lambda b, h, i: (b * nq + i, h))
    kv_spec = pl.BlockSpec((seq, hp * d), lambda b, h, i: (b, h))
    r_in, r_out, r_shape, r_blocks = _rounding_specs(rounding, batch * ng * nq,
                                                     lambda b, h, i: (b * ng + h) * nq + i)
    res = pl.pallas_call(
        functools.partial(_sb_prompt_body, tq=tq, hp=hp),
        grid=(batch, ng, nq),
        in_specs=[pl.BlockSpec((hp, 1, tq), lambda b, h, i: (h, 0, 0)), q_spec, kv_spec, kv_spec] + r_in,
        out_specs=[q_spec] + r_out,
        out_shape=[jax.ShapeDtypeStruct((batch * seq, heads * d), BF16)] + r_shape,
        compiler_params=_params(("arbitrary", "arbitrary", "arbitrary"),
                                [((tq, hp * d), BF16)] * 2 + [((seq, hp * d), BF16)] * 2 + r_blocks),
        name=name,
    )(bias_rows, q, k, v, *[w for w, _ in rounding])
    return res[0] if not rounding else tuple(res)


def _sb_sample_body(pt_ref, q_ref, bias_ref, *refs, heads, pps, scale):
    del pt_ref
    k_refs, v_refs = refs[:pps], refs[pps:2 * pps]
    o_ref, acc_ref, carry_ref = refs[2 * pps:]
    pg = pl.program_id(1)
    page = k_refs[0].shape[0]
    d = LANES

    @pl.when(pg == 0)
    def _():
        acc_ref[...] = jnp.zeros_like(acc_ref)
        carry_ref[...] = jnp.zeros_like(carry_ref)

    row = lax.broadcasted_iota(jnp.int32, (page, page), 0)
    col = lax.broadcasted_iota(jnp.int32, (page, page), 1)
    later = (row > col).astype(BF16)
    hrow = lax.broadcasted_iota(jnp.int32, (heads, page), 0)
    hrow_d = lax.broadcasted_iota(jnp.int32, (heads, d), 0)
    q = q_ref[...]
    bias = bias_ref[...]
    acc = acc_ref[...]
    carry = carry_ref[...]
    for k_ref, v_ref in zip(k_refs, v_refs):
        kt = jnp.swapaxes(k_ref[...], 0, 1).astype(BF16)
        vt = jnp.swapaxes(v_ref[...], 0, 1).astype(BF16)
        z = jnp.zeros((heads, page), F32)
        for h in range(heads):
            zh = lax.dot_general(q, kt[h], NT_DIMS, preferred_element_type=F32)
            z = jnp.where(hrow == h, zh, z)
        z = z * scale + bias
        log_keep = _neg_softplus(z)
        after = _sum_over_later(log_keep, later, 3)
        w = jnp.exp(log_keep + z + after + carry).astype(BF16)
        carry = carry + after[:, :1] + log_keep[:, :1]
        for h in range(heads):
            oh = jnp.dot(w, vt[h], preferred_element_type=F32)
            acc = acc + jnp.where(hrow_d == h, oh, 0.0)
    acc_ref[...] = acc
    carry_ref[...] = carry

    @pl.when(pg == pl.num_programs(1) - 1)
    def _():
        o_ref[...] = acc.astype(o_ref.dtype)


def _sb_sample(q, bias, cache_k, cache_v, page_table, layer, *, pps, name):
    bsz, heads, d = q.shape
    page = cache_k.shape[2]
    n_pages = page_table.shape[1]
    assert n_pages % pps == 0
    bias_rows = jnp.broadcast_to(bias.astype(F32)[:, None], (heads, page))

    def cache_spec(r):
        return pl.BlockSpec((None, None, page, heads, d),
                            lambda b, p, pt: (pt[b, n_pages - 1 - (p * pps + r)], layer, 0, 0, 0))

    grid_spec = pltpu.PrefetchScalarGridSpec(
        num_scalar_prefetch=1,
        grid=(bsz, n_pages // pps),
        in_specs=[pl.BlockSpec((None, heads, d), lambda b, p, pt: (b, 0, 0)),
                  pl.BlockSpec((heads, page), lambda b, p, pt: (0, 0))]
                 + [cache_spec(r) for r in range(pps)] * 2,
        out_specs=pl.BlockSpec((None, heads, d), lambda b, p, pt: (b, 0, 0)),
        scratch_shapes=[pltpu.VMEM((heads, d), F32), pltpu.VMEM((heads, 1), F32)],
    )
    out = pl.pallas_call(
        functools.partial(_sb_sample_body, heads=heads, pps=pps, scale=d ** -0.5),
        grid_spec=grid_spec,
        out_shape=jax.ShapeDtypeStruct((bsz, heads, d), BF16),
        compiler_params=_params(("parallel", "arbitrary"), [((page, heads, d), F32)] * (2 * pps)),
        name=name,
    )(page_table, q.astype(BF16), bias_rows, *([cache_k] * pps), *([cache_v] * pps))
    return out.reshape(bsz, heads * d)


def _gmlp_prompt_body(u_ref, vb_ref, gn_ref, ws_ref, bst_ref, o_ref, *, groups):
    chunk = u_ref.shape[0]
    vn = _rms_rows(jax.nn.gelu(vb_ref[...]), gn_ref[...])
    row = lax.broadcasted_iota(jnp.int32, (chunk, chunk), 0)
    col = lax.broadcasted_iota(jnp.int32, (chunk, chunk), 1)
    for g in range(groups):
        sl = slice(g * LANES, (g + 1) * LANES)
        w = jnp.where(col <= row, ws_ref[g], 0.0).astype(BF16)
        s = jnp.dot(w, vn[:, sl].astype(BF16), preferred_element_type=F32) + bst_ref[:, g:g + 1]
        o_ref[:, sl] = (jax.nn.gelu(u_ref[:, sl]) * s).astype(o_ref.dtype)


def _gmlp_prompt(p, gnorm, ws, bs, *, u_col, name):
    groups, chunk, _ = ws.shape
    width = groups * LANES
    m = p.shape[0]
    return pl.pallas_call(
        functools.partial(_gmlp_prompt_body, groups=groups),
        grid=(m // chunk,),
        in_specs=[pl.BlockSpec((chunk, width), lambda i: (i, u_col)),
                  pl.BlockSpec((chunk, width), lambda i: (i, u_col + 1)),
                  pl.BlockSpec((1, width), lambda i: (0, 0)),
                  pl.BlockSpec((groups, chunk, chunk), lambda i: (0, 0, 0)),
                  pl.BlockSpec((chunk, groups), lambda i: (0, 0))],
        out_specs=pl.BlockSpec((chunk, width), lambda i: (i, 0)),
        out_shape=jax.ShapeDtypeStruct((m, width), BF16),
        compiler_params=_params(("parallel",),
                                [((chunk, width), F32)] * 2 + [((groups, chunk, chunk), F32), ((chunk, width), BF16)]),
        name=name,
    )(p, p, gnorm.reshape(1, width), ws, bs.T)


def _gmlp_sample_body(u_ref, vb_ref, gn_ref, w0_ref, b0_ref, o_ref, vn_ref):
    vn = _rms_rows(jax.nn.gelu(vb_ref[...]), gn_ref[...])
    vn_ref[...] = vn
    o_ref[...] = (jax.nn.gelu(u_ref[...]) * (w0_ref[...] * vn + b0_ref[...])).astype(o_ref.dtype)


def _gmlp_sample(p, gnorm, ws, bs, *, u_col, name):
    groups = ws.shape[0]
    width = groups * LANES
    m = p.shape[0]
    w0 = jnp.repeat(ws[:, 0, 0], LANES).reshape(1, width)
    b0 = jnp.repeat(bs[:, 0], LANES).reshape(1, width)
    row = lambda c: pl.BlockSpec((m, width), lambda i: (0, c))
    vec = pl.BlockSpec((1, width), lambda i: (0, 0))
    return pl.pallas_call(
        _gmlp_sample_body,
        grid=(1,),
        in_specs=[row(u_col), row(u_col + 1), vec, vec, vec],
        out_specs=[row(0), row(0)],
        out_shape=[jax.ShapeDtypeStruct((m, width), BF16), jax.ShapeDtypeStruct((m, width), F32)],
        compiler_params=_params(("arbitrary",), [((m, width), F32)] * 4),
        name=name,
    )(p, p, gnorm.reshape(1, width), w0, b0)


HGRN_LEVELS = 7


def _hgrn_gates(fpre, lb):
    t = jnp.log(1.0 + jnp.exp(-jnp.abs(fpre)))
    log_sig = jnp.minimum(fpre, 0.0) - t
    log_sig_neg = jnp.minimum(-fpre, 0.0) - t
    a = jnp.log(lb)
    c = jnp.log1p(-lb) + log_sig
    logf = jnp.maximum(a, c) + jnp.log(1.0 + jnp.exp(-jnp.abs(a - c)))
    return logf, (1.0 - lb) * jnp.exp(log_sig_neg)


def _hgrn_finish(o, g, gn):
    return _rms_rows(o, gn) * jax.nn.silu(g)


def _hgrn_tables():
    n = LANES
    t = np.arange(n)[:, None]
    r = np.arange(n)[None, :]
    blocks = []
    for level in range(HGRN_LEVELS):
        c = 1 << level
        mid = (t // (2 * c)) * (2 * c) + c
        second = (t % (2 * c)) >= c
        blocks.append(np.where(second, (r >= mid) & (r <= t), (r > t) & (r < mid)))
    blocks += [r <= t, r > t, np.ones((SUBLANES, n), bool)]
    ranges = np.concatenate(blocks, axis=0).astype(np.float32)
    diff = t ^ r
    level_of = np.where(t == r, -1, np.where(r < t, np.floor(np.log2(np.maximum(diff, 1))), HGRN_LEVELS))
    return (jnp.asarray(np.concatenate([ranges, ranges], axis=1), BF16),
            jnp.asarray(level_of.astype(np.int32)))


def _hgrn_prompt_body(q_ref, f_ref, i_ref, g_ref, lb_ref, gn_ref, rng_ref, lvl_ref, *refs, tb):
    n_round = (len(refs) - 3) // 2
    o_ref, s_ref = refs[n_round:n_round + 2]
    st_ref = refs[-1]
    _round_blocks(refs[:n_round], refs[n_round + 2:-1])
    step = pl.program_id(2)
    n = LANES

    @pl.when(step == 0)
    def _():
        st_ref[...] = jnp.zeros_like(st_ref)

    lb = lb_ref[...]
    gn = gn_ref[...]
    level_of = lvl_ref[...]
    st = st_ref[...]
    tiles = [slice(i * n, (i + 1) * n) for i in range(tb // n)]
    logf_all, k_all = _hgrn_gates(f_ref[...], lb)
    rhs = jnp.concatenate([jnp.concatenate([piece[rs] for rs in tiles], axis=1)
                           for piece in _bf16_pieces(logf_all, 2)], axis=0)
    decay_all = jnp.exp(jnp.dot(rng_ref[...], rhs, preferred_element_type=F32))
    for rs in tiles:
        q = q_ref[rs, :]
        v = i_ref[rs, :]
        k = k_all[rs]
        decay = decay_all[:, rs]
        qb, kb, vb = q.astype(BF16), k.astype(BF16), v.astype(BF16)
        scores = jnp.zeros((n, n), F32)
        for level in range(HGRN_LEVELS):
            dl = decay[level * n:(level + 1) * n].astype(BF16)
            pair = lax.dot_general(qb * dl, kb * dl, NT_DIMS, preferred_element_type=F32)
            scores = jnp.where(level_of == level, pair, scores)
        same = lax.dot_general(qb, kb, NT_DIMS, preferred_element_type=F32)
        scores = jnp.where(level_of == -1, same, scores)
        from_start = decay[HGRN_LEVELS * n:(HGRN_LEVELS + 1) * n].astype(BF16)
        to_end = decay[(HGRN_LEVELS + 1) * n:(HGRN_LEVELS + 2) * n].astype(BF16)
        total = decay[(HGRN_LEVELS + 2) * n:(HGRN_LEVELS + 2) * n + 1]
        o = (jnp.dot(scores.astype(BF16), vb, preferred_element_type=F32)
             + lax.dot_general(qb * from_start, st.astype(BF16), NT_DIMS, preferred_element_type=F32))
        st = st * total + lax.dot_general(vb, kb * to_end, TN_DIMS, preferred_element_type=F32)
        o_ref[rs, :] = _hgrn_finish(o, g_ref[rs, :], gn).astype(o_ref.dtype)
    st_ref[...] = st

    @pl.when(step == pl.num_programs(2) - 1)
    def _():
        s_ref[...] = st.T


def _hgrn_prompt(p, lb, gnorm, *, batch, seq, heads, tb, rounding=(), name):
    d = LANES
    nt = seq // tb
    ranges, level_of = _hgrn_tables()
    col = lambda c: pl.BlockSpec((tb, d), lambda b, h, t: (b * nt + t, c * heads + h))
    r_in, r_out, r_shape, r_blocks = _rounding_specs(rounding, batch * heads * nt,
                                                     lambda b, h, t: (b * heads + h) * nt + t)
    res = pl.pallas_call(
        functools.partial(_hgrn_prompt_body, tb=tb),
        grid=(batch, heads, nt),
        in_specs=[col(0), col(1), col(2), col(3),
                  pl.BlockSpec((None, 1, d), lambda b, h, t: (h, 0, 0)),
                  pl.BlockSpec((1, d), lambda b, h, t: (0, 0)),
                  pl.BlockSpec(ranges.shape, lambda b, h, t: (0, 0)),
                  pl.BlockSpec(level_of.shape, lambda b, h, t: (0, 0))] + r_in,
        out_specs=[pl.BlockSpec((tb, d), lambda b, h, t: (b * nt + t, h)),
                   pl.BlockSpec((None, None, d, d), lambda b, h, t: (b, h, 0, 0))] + r_out,
        out_shape=[jax.ShapeDtypeStruct((batch * seq, heads * d), BF16),
                   jax.ShapeDtypeStruct((batch, heads, d, d), F32)] + r_shape,
        scratch_shapes=[pltpu.VMEM((d, d), F32)],
        compiler_params=_params(("arbitrary", "arbitrary", "arbitrary"),
                                [((tb, d), F32)] * 5 + [((d, d), F32), (ranges.shape, BF16)] + r_blocks),
        name=name,
    )(p, p, p, p, lb.reshape(heads, 1, d), gnorm.reshape(1, d), ranges, level_of, *[w for w, _ in rounding])
    return tuple(res)


def _hgrn_sample_body(q_ref, f_ref, i_ref, g_ref, lb_ref, gn_ref, s_ref, o_ref, so_ref, oacc_ref, *, heads):
    d = LANES
    q = q_ref[...]
    v = i_ref[...]
    logf, k = _hgrn_gates(f_ref[...], lb_ref[...])
    f = jnp.exp(logf)
    eye = lax.broadcasted_iota(jnp.int32, (d, d), 0) == lax.broadcasted_iota(jnp.int32, (d, d), 1)

    def to_col(r):
        return jnp.sum(jnp.where(eye, r, 0.0), axis=1, keepdims=True)

    for h in range(heads):
        hs = slice(h, h + 1)
        s_new = to_col(f[hs]) * s_ref[h] + to_col(k[hs]) * v[hs]
        so_ref[h] = s_new
        oacc_ref[hs, :] = jnp.sum(to_col(q[hs]) * s_new, axis=0, keepdims=True)
    o_ref[...] = _hgrn_finish(oacc_ref[...], g_ref[...], gn_ref[...]).astype(o_ref.dtype)


def _hgrn_sample(p, lb, gnorm, state, layer, *, heads, name):
    bsz = p.shape[0]
    d = LANES
    p4 = p.reshape(bsz, 4, heads, d)
    part = lambda c: pl.BlockSpec((None, None, heads, d), lambda b: (b, c, 0, 0))
    o, s_new = pl.pallas_call(
        functools.partial(_hgrn_sample_body, heads=heads),
        grid=(bsz,),
        in_specs=[part(0), part(1), part(2), part(3),
                  pl.BlockSpec((heads, d), lambda b: (0, 0)),
                  pl.BlockSpec((1, d), lambda b: (0, 0)),
                  pl.BlockSpec((None, None, heads, d, d), lambda b: (layer, b, 0, 0, 0))],
        out_specs=[pl.BlockSpec((None, heads, d), lambda b: (b, 0, 0)),
                   pl.BlockSpec((None, heads, d, d), lambda b: (b, 0, 0, 0))],
        out_shape=[jax.ShapeDtypeStruct((bsz, heads, d), BF16),
                   jax.ShapeDtypeStruct((bsz, heads, d, d), F32)],
        scratch_shapes=[pltpu.VMEM((heads, d), F32)],
        compiler_params=_params(("parallel",), [((heads, d, d), F32)] * 2),
        name=name,
    )(p4, p4, p4, p4, lb.reshape(heads, d), gnorm.reshape(1, d), state)
    return o.reshape(bsz, heads * d), s_new


def _attend(q, k_ref, v_ref, heads, scale):
    outs = []
    for h in range(heads):
        sl = slice(h * LANES, (h + 1) * LANES)
        s = lax.dot_general(q[:, sl].astype(BF16), k_ref[:, sl].astype(BF16), NT_DIMS,
                            preferred_element_type=F32) * scale
        e = jnp.exp(s - jnp.max(s, axis=-1, keepdims=True))
        prob = e * (1.0 / jnp.sum(e, axis=-1, keepdims=True))
        outs.append(jnp.dot(prob.astype(BF16), v_ref[:, sl].astype(BF16), preferred_element_type=F32))
    return jnp.concatenate(outs, axis=1)


def _xattn_prompt_body(x_ref, gx_ref, gf_ref, wq_ref, wo_ref, k_ref, v_ref, y_ref, hf_ref, *, heads, scale):
    x = x_ref[...]
    h = _rms_rows(x, gx_ref[...]).astype(BF16)
    q = jnp.dot(h, wq_ref[...], preferred_element_type=F32)
    att = _attend(q, k_ref, v_ref, heads, scale).astype(BF16)
    y = x + jnp.dot(att, wo_ref[...], preferred_element_type=F32)
    y_ref[...] = y
    hf_ref[...] = _rms_rows(y, gf_ref[...]).astype(hf_ref.dtype)


def _xattn_prompt(x, gx, gf, wq, wo, layer, mem_k, mem_v, *, batch, seq, tq, name):
    d = x.shape[1]
    xw = wq.shape[2]
    heads = xw // LANES
    mlen = mem_k.shape[1]
    nt = seq // tq
    rows = pl.BlockSpec((tq, d), lambda b, t: (b * nt + t, 0))
    vec = pl.BlockSpec((1, d), lambda b, t: (0, 0))
    mem = pl.BlockSpec((None, mlen, xw), lambda b, t: (b, 0, 0))
    return pl.pallas_call(
        functools.partial(_xattn_prompt_body, heads=heads, scale=LANES ** -0.5),
        grid=(batch, nt),
        in_specs=[rows, vec, vec,
                  pl.BlockSpec((None, d, xw), lambda b, t: (layer, 0, 0)),
                  pl.BlockSpec((None, xw, d), lambda b, t: (layer, 0, 0)),
                  mem, mem],
        out_specs=[rows, rows],
        out_shape=[jax.ShapeDtypeStruct(x.shape, F32), jax.ShapeDtypeStruct(x.shape, BF16)],
        compiler_params=_params(("parallel", "parallel"),
                                [((tq, d), F32)] * 2 + [((tq, d), BF16), ((d, xw), BF16), ((xw, d), BF16),
                                                        ((mlen, xw), F32), ((mlen, xw), F32)]),
        name=name,
    )(x, gx.reshape(1, d), gf.reshape(1, d), wq, wo, mem_k, mem_v)


def _xattn_sample_body(x_ref, gx_ref, gf_ref, wq_ref, wo_ref, k_ref, v_ref, y_ref, hf_ref, q_sc, att_sc,
                       *, heads, scale):
    b = pl.program_id(0)

    @pl.when(b == 0)
    def _():
        h = _rms_rows(x_ref[...], gx_ref[...]).astype(BF16)
        q_sc[...] = jnp.dot(h, wq_ref[...], preferred_element_type=F32)
        att_sc[...] = jnp.zeros_like(att_sc)

    att = _attend(q_sc[...], k_ref, v_ref, heads, scale)
    mine = lax.broadcasted_iota(jnp.int32, att.shape, 0) == b
    att_sc[...] += jnp.where(mine, att, 0.0)

    @pl.when(b == pl.num_programs(0) - 1)
    def _():
        y = x_ref[...] + jnp.dot(att_sc[...].astype(BF16), wo_ref[...], preferred_element_type=F32)
        y_ref[...] = y
        hf_ref[...] = _rms_rows(y, gf_ref[...]).astype(hf_ref.dtype)


def _xattn_sample(x, gx, gf, wq, wo, mem_k, mem_v, layer, name):
    bsz, d = x.shape
    xw = wq.shape[2]
    heads = xw // LANES
    mlen = mem_k.shape[2]
    mk = mem_k.reshape(mem_k.shape[0], bsz, mlen, xw)
    mv = mem_v.reshape(mem_v.shape[0], bsz, mlen, xw)
    rows = pl.BlockSpec((bsz, d), lambda b: (0, 0))
    vec = pl.BlockSpec((1, d), lambda b: (0, 0))
    mem = pl.BlockSpec((None, None, mlen, xw), lambda b: (layer, b, 0, 0))
    return pl.pallas_call(
        functools.partial(_xattn_sample_body, heads=heads, scale=LANES ** -0.5),
        grid=(bsz,),
        in_specs=[rows, vec, vec,
                  pl.BlockSpec((None, d, xw), lambda b: (layer, 0, 0)),
                  pl.BlockSpec((None, xw, d), lambda b: (layer, 0, 0)),
                  mem, mem],
        out_specs=[rows, rows],
        out_shape=[jax.ShapeDtypeStruct(x.shape, F32), jax.ShapeDtypeStruct(x.shape, BF16)],
        scratch_shapes=[pltpu.VMEM((bsz, xw), F32), pltpu.VMEM((bsz, xw), F32)],
        compiler_params=_params(("arbitrary",),
                                [((d, xw), BF16), ((xw, d), BF16), ((mlen, xw), F32), ((mlen, xw), F32)]),
        name=name,
    )(x, gx.reshape(1, d), gf.reshape(1, d), wq, wo, mk, mv)


def kernel(x_prompt, x_sample, mem_prompt, cache_sb_k, cache_sb_v, cache_mem_k, cache_mem_v, state_hgrn, page_table, norm_mix, norm_mem, norm_xattn, norm_ffn, norm_final, w_in_ab, w_out_ab, sb_bias, gmlp_norm, gmlp_ws, gmlp_bs, w_in_c, w_out_c, hgrn_lb, hgrn_gnorm, xattn_wq, xattn_wkv, xattn_wo, ffn_w13, ffn_w2):
    bp, tp, d = x_prompt.shape
    bd, ts, _ = x_sample.shape
    assert ts == 1, "the sample group carries one new token per sequence"
    depth = norm_mix.shape[0]
    n_even = w_in_ab.shape[0]
    a_heads = sb_bias.shape[1]
    a_width = a_heads * LANES
    groups = gmlp_ws.shape[1]
    b_width = groups * LANES
    c_heads = hgrn_lb.shape[1] // LANES
    mem_len = mem_prompt.shape[1]
    xw = xattn_wq.shape[2]
    assert a_width == b_width and w_in_ab.shape[2] == 3 * a_width + 2 * b_width

    xp = x_prompt.reshape(bp * tp, d)
    xs = x_sample.reshape(bd, d)
    mem = mem_prompt.reshape(bp * mem_len, d)

    lb_all = jnp.cumsum(jax.nn.softmax(hgrn_lb.astype(F32), axis=0), axis=0)
    lb_all = lb_all - lb_all[:1]

    tm_p, tm_s = 1024, bd
    ffn_tn = 256

    w_in = w_in_ab[0].astype(BF16)
    wq_all = xattn_wq.astype(BF16)
    wo_all = xattn_wo.astype(BF16)
    wkv_all = xattn_wkv.astype(BF16)

    sbk_p, sbv_p = None, None
    sbk_s, sbv_s, gv_s, hs_p, hs_s, mk_p, mv_p = [], [], [], [], [], [], []
    for l in range(depth):
        j = l // 2
        hp = _rmsnorm(xp, norm_mix[l], BF16, "norm_mix_p")
        hs = _rmsnorm(xs, norm_mix[l], BF16, "norm_mix_s")
        if l % 2 == 0:
            qp = _matmul([hp], w_in, w_cols=(0, a_width), out_dtype=BF16, tm=tm_p, tn=1024,
                         out_scale=LANES ** -0.5, name="in_q_p")
            kp2, sbk_p = _matmul_heads(hp, w_in, layer=None, w_cols=(a_width, a_width), batch=bp, seq=tp, slot=j,
                                       n_slots=n_even, stacked=sbk_p, tm=tm_p, name="in_k_p")
            vp2, sbv_p = _matmul_heads(hp, w_in, layer=None, w_cols=(2 * a_width, a_width), batch=bp, seq=tp,
                                       slot=j, n_slots=n_even, stacked=sbv_p, tm=tm_p, name="in_v_p")
            uv = _matmul([hp], w_in, w_cols=(3 * a_width, 2 * b_width), out_dtype=F32, tm=tm_p, tn=1024,
                         name="in_uv_p")
            oa, w13, w_out = _sb_prompt(qp, kp2, vp2, sb_bias[j], batch=bp, seq=tp, heads=a_heads, tq=512, hp=4,
                                        rounding=[(ffn_w13, l), (w_out_ab, j)], name="sb_p")
            ob = _gmlp_prompt(uv, gmlp_norm[j], gmlp_ws[j], gmlp_bs[j], u_col=0, name="gmlp_p")
            ps = _matmul([hs], w_in, out_dtype=F32, tm=tm_s, tn=1024, name="in_ab_s")
            oa_s = _sb_sample(ps[:, :a_width].reshape(bd, a_heads, LANES), sb_bias[j], cache_sb_k, cache_sb_v,
                              page_table, j, pps=8, name="sb_s")
            ob_s, vn = _gmlp_sample(ps, gmlp_norm[j], gmlp_ws[j], gmlp_bs[j], u_col=3 * a_width // b_width,
                                    name="gmlp_s")
            xp, xs = _matmul([oa, ob], w_out, out_dtype=F32, tm=tm_p, tn=1024, resid=xp,
                             sample=([oa_s, ob_s], xs), name="out_ab")
            sbk_s.append(ps[:, a_width:2 * a_width].reshape(bd, ts, a_heads, LANES))
            sbv_s.append(ps[:, 2 * a_width:3 * a_width].reshape(bd, ts, a_heads, LANES))
            gv_s.append(vn.reshape(bd, ts, groups, LANES))
        else:
            pp, ps = _matmul([hp], w_in, out_dtype=F32, tm=tm_p, tn=1024, sample=([hs], None), name="in_c")
            oc, s_p, w13, w_out = _hgrn_prompt(pp, lb_all[j], hgrn_gnorm[j], batch=bp, seq=tp, heads=c_heads,
                                               tb=1024, rounding=[(ffn_w13, l), (w_out_c, j)], name="hgrn_p")
            oc_s, s_s = _hgrn_sample(ps, lb_all[j], hgrn_gnorm[j], state_hgrn, j, heads=c_heads, name="hgrn_s")
            xp, xs = _matmul([oc], w_out, out_dtype=F32, tm=tm_p, tn=1024, resid=xp, sample=([oc_s], xs),
                             name="out_c")
            hs_p.append(s_p)
            hs_s.append(s_s)
        hm = _rmsnorm(mem, norm_mem[l], BF16, "norm_mem")
        kv = _matmul([hm], wkv_all, layer=l, out_dtype=F32, tm=bp * mem_len, tn=512, name="mem_kv")
        kp = kv[:, :xw].reshape(bp, mem_len, xw)
        vp = kv[:, xw:].reshape(bp, mem_len, xw)
        mk_p.append(kp.reshape(bp, mem_len, xw // LANES, LANES))
        mv_p.append(vp.reshape(bp, mem_len, xw // LANES, LANES))
        xp, hfp = _xattn_prompt(xp, norm_xattn[l], norm_ffn[l], wq_all, wo_all, l, kp, vp, batch=bp, seq=tp, tq=512,
                                name="xattn_p")
        xs, hfs = _xattn_sample(xs, norm_xattn[l], norm_ffn[l], wq_all, wo_all, cache_mem_k, cache_mem_v, l,
                                "xattn_s")
        nxt = [] if l + 1 == depth else [(w_in_c, (l + 1) // 2)] if l % 2 == 0 else [(w_in_ab, (l + 1) // 2)]
        mid, mid_s, w2, *w_next = _matmul_swiglu(hfp, w13, tm=2 * tm_p, tn=ffn_tn, sample=hfs,
                                                 rounding=[(ffn_w2, l)] + nxt, name="ffn13")
        if w_next:
            w_in = w_next[0]
        xp, xs = _matmul([mid], w2, out_dtype=F32, tm=512, tn=1024, resid=xp, w_single=True,
                         sample=([mid_s], xs), name="ffn2")
    y_prompt = _rmsnorm(xp, norm_final, F32, "norm_final_p").reshape(bp, tp, d)
    y_sample = _rmsnorm(xs, norm_final, F32, "norm_final_s").reshape(bd, ts, d)
    return (y_prompt, y_sample, sbk_p, sbv_p,
            jnp.stack(sbk_s, axis=1), jnp.stack(sbv_s, axis=1),
            jnp.stack(gv_s, axis=1),
            jnp.stack(hs_p, axis=0), jnp.stack(hs_s, axis=0),
            jnp.stack(mk_p, axis=0), jnp.stack(mv_p, axis=0))
```

```python
import functools

import numpy as np

import jax
import jax.numpy as jnp
from jax import lax
from jax.experimental import pallas as pl
from jax.experimental.pallas import tpu as pltpu

F32 = jnp.float32
BF16 = jnp.bfloat16
EPS = 1e-6
LANES = 128
SUBLANES = 8
MXU_WIDTH = 256
VMEM_CAP = 60 * 1024 * 1024
VMEM_SLACK = 12 * 1024 * 1024
NT_DIMS = (((1,), (1,)), ((), ()))
TN_DIMS = (((0,), (0,)), ((), ()))


def _nbytes(shape, dtype):
    n = 1
    for s in shape:
        n *= s
    return n * jnp.dtype(dtype).itemsize


def _params(semantics, blocks):
    need = sum(_nbytes(b[0], b[1]) * (b[2] if len(b) > 2 else 2) for b in blocks) + VMEM_SLACK
    return pltpu.CompilerParams(dimension_semantics=semantics,
                                vmem_limit_bytes=int(min(need, VMEM_CAP)))


def _rms_rows(x, g):
    r = lax.rsqrt(jnp.mean(x * x, axis=-1, keepdims=True) + EPS)
    return (x * r) * g


def _bf16_pieces(x, terms):
    out = []
    for _ in range(terms):
        piece = x.astype(BF16)
        out.append(piece)
        x = x - piece.astype(F32)
    return out


def _neg_softplus(z):
    return -(jnp.maximum(z, 0.0) + jnp.log(1.0 + jnp.exp(-jnp.abs(z))))


def _sum_over_later(log_keep, later, terms):
    w = later.shape[0]
    groups = [log_keep[:, g * w:(g + 1) * w] for g in range(log_keep.shape[1] // w)]
    inside = []
    for lk in groups:
        acc = None
        for piece in _bf16_pieces(lk, terms):
            part = jnp.dot(piece, later, preferred_element_type=F32)
            acc = part if acc is None else acc + part
        inside.append(acc)
    out, beyond = [], None
    for lk, acc in zip(reversed(groups), reversed(inside)):
        out.append(acc if beyond is None else acc + beyond)
        total = acc[:, :1] + lk[:, :1]
        beyond = total if beyond is None else beyond + total
    return jnp.concatenate(out[::-1], axis=1) if len(out) > 1 else out[0]


def _norm_body(x_ref, g_ref, o_ref):
    o_ref[...] = _rms_rows(x_ref[...], g_ref[...]).astype(o_ref.dtype)


def _rmsnorm(x, g, out_dtype, name):
    m, d = x.shape
    tm = min(m, 256)
    return pl.pallas_call(
        _norm_body,
        grid=(m // tm,),
        in_specs=[pl.BlockSpec((tm, d), lambda i: (i, 0)),
                  pl.BlockSpec((1, d), lambda i: (0, 0))],
        out_specs=pl.BlockSpec((tm, d), lambda i: (i, 0)),
        out_shape=jax.ShapeDtypeStruct((m, d), out_dtype),
        compiler_params=_params(("parallel",), [((tm, d), F32), ((tm, d), out_dtype)]),
        name=name,
    )(x, g.reshape(1, d))


def _w_spec(w, layer, k, tn, col_of, **mode):
    if w.ndim == 3:
        return pl.BlockSpec((None, k, tn), lambda j, i: (layer, 0, col_of(j)), **mode)
    return pl.BlockSpec((k, tn), lambda j, i: (0, col_of(j)), **mode)


BF16_ROWS = 16


def _rounding_specs(jobs, n_steps, step_of):
    in_specs, out_specs, out_shape, blocks = [], [], [], []
    for w, layer in jobs:
        rows, cols = w.shape[-2:]
        r = BF16_ROWS
        while rows % r or rows // r > n_steps:
            r += BF16_ROWS
        last = rows // r - 1
        in_specs.append(pl.BlockSpec((None, r, cols),
                                     lambda *g, layer=layer, last=last: (layer, jnp.minimum(step_of(*g), last), 0)))
        out_specs.append(pl.BlockSpec((r, cols), lambda *g, last=last: (jnp.minimum(step_of(*g), last), 0)))
        out_shape.append(jax.ShapeDtypeStruct((rows, cols), BF16))
        blocks += [((r, cols), F32), ((r, cols), BF16)]
    return in_specs, out_specs, out_shape, blocks


def _round_blocks(src_refs, dst_refs):
    for src, dst in zip(src_refs, dst_refs):
        dst[...] = src[...].astype(BF16)


def _project(a_refs, w_ref, k_sizes):
    acc, off = None, 0
    for a_ref, kk in zip(a_refs, k_sizes):
        part = jnp.dot(a_ref[...], w_ref[off:off + kk, :], preferred_element_type=F32)
        acc = part if acc is None else acc + part
        off += kk
    return acc


def _mm_body(*refs, k_sizes, has_resid, has_sample, out_scale):
    n_a = len(k_sizes)
    a_refs, w_ref = refs[:n_a], refs[n_a]
    pos = n_a + 1
    r_ref = refs[pos] if has_resid else None
    pos += has_resid
    acc = _project(a_refs, w_ref, k_sizes)
    if out_scale is not None:
        acc = acc * out_scale
    if has_resid:
        acc = acc + r_ref[...]
    o_ref = refs[-2] if has_sample else refs[-1]
    o_ref[...] = acc.astype(o_ref.dtype)
    if has_sample:
        s_refs = refs[pos:pos + n_a]
        rs_ref = refs[pos + n_a] if has_resid else None
        os_ref = refs[-1]

        @pl.when(pl.program_id(1) == 0)
        def _():
            acc_s = _project(s_refs, w_ref, k_sizes)
            if has_resid:
                acc_s = acc_s + rs_ref[...]
            os_ref[...] = acc_s.astype(os_ref.dtype)


def _matmul(a_list, w, *, out_dtype, tm, tn, layer=None, resid=None, w_cols=None, w_single=False,
            sample=None, out_scale=None, name):
    m = a_list[0].shape[0]
    k_sizes = tuple(a.shape[1] for a in a_list)
    k = w.shape[-2]
    first, n = w_cols if w_cols is not None else (0, w.shape[-1])
    assert sum(k_sizes) == k and m % tm == 0 and n % tn == 0 and first % tn == 0
    assert out_scale is None or (resid is None and sample is None)
    j0 = first // tn
    in_specs = [pl.BlockSpec((tm, kk), lambda j, i: (i, 0)) for kk in k_sizes]
    w_mode = dict(pipeline_mode=pl.Buffered(1)) if w_single else {}
    in_specs.append(_w_spec(w, layer, k, tn, lambda j: j0 + j, **w_mode))
    blocks = [((tm, kk), BF16) for kk in k_sizes] + [((k, tn), w.dtype, 1 if w_single else 2), ((tm, tn), out_dtype)]
    args = list(a_list) + [w]
    if resid is not None:
        in_specs.append(pl.BlockSpec((tm, tn), lambda j, i: (i, j)))
        blocks.append(((tm, tn), F32))
        args.append(resid)
    out_specs = [pl.BlockSpec((tm, tn), lambda j, i: (i, j))]
    out_shape = [jax.ShapeDtypeStruct((m, n), out_dtype)]
    if sample is not None:
        a_s_list, resid_s = sample
        ms = a_s_list[0].shape[0]
        assert (resid_s is None) == (resid is None) and tuple(a.shape[1] for a in a_s_list) == k_sizes
        in_specs += [pl.BlockSpec((ms, kk), lambda j, i: (0, 0)) for kk in k_sizes]
        args += list(a_s_list)
        if resid_s is not None:
            in_specs.append(pl.BlockSpec((ms, tn), lambda j, i: (0, j)))
            args.append(resid_s)
        out_specs.append(pl.BlockSpec((ms, tn), lambda j, i: (0, j)))
        out_shape.append(jax.ShapeDtypeStruct((ms, n), out_dtype))
        blocks += [((ms, k), BF16), ((ms, tn), F32), ((ms, tn), F32)]
    res = pl.pallas_call(
        functools.partial(_mm_body, k_sizes=k_sizes, has_resid=resid is not None, has_sample=sample is not None,
                          out_scale=out_scale),
        grid=(n // tn, m // tm),
        in_specs=in_specs,
        out_specs=out_specs,
        out_shape=out_shape,
        compiler_params=_params(("parallel", "arbitrary"), blocks),
        name=name,
    )(*args)
    return res[0] if sample is None else tuple(res)


def _mm_heads_body(a_ref, w_ref, *rest, hb, slot):
    o2_ref, o5_ref = rest[-2], rest[-1]
    acc = jnp.dot(a_ref[...], w_ref[...], preferred_element_type=F32)
    o2_ref[...] = acc.astype(o2_ref.dtype)
    heads = acc.reshape(acc.shape[0], hb, LANES)
    if len(o5_ref.shape) == 3:
        o5_ref[...] = heads
    else:
        for s in range(o5_ref.shape[0]):
            o5_ref[s] = heads if s == slot else jnp.zeros_like(heads)


def _matmul_heads(a, w, *, layer, w_cols, batch, seq, slot, n_slots, stacked, tm, name):
    m, k = a.shape
    first, n = w_cols
    heads = n // LANES
    hb = SUBLANES
    tn = hb * LANES
    tps = seq // tm
    assert m == batch * seq and seq % tm == 0 and n % tn == 0 and first % tn == 0
    j0 = first // tn
    in_specs = [pl.BlockSpec((tm, k), lambda j, i: (i, 0)),
                _w_spec(w, layer, k, tn, lambda j: j0 + j)]
    args = [a, w]
    if stacked is None:
        aliases = {}
        o5_spec = pl.BlockSpec((None, n_slots, tm, hb, LANES), lambda j, i: (i // tps, 0, i % tps, j, 0))
        o5_block = ((n_slots, tm, tn), F32)
    else:
        in_specs.append(pl.BlockSpec(memory_space=pl.ANY))
        args.append(stacked)
        aliases = {2: 1}
        o5_spec = pl.BlockSpec((None, None, tm, hb, LANES), lambda j, i: (i // tps, slot, i % tps, j, 0))
        o5_block = ((tm, tn), F32)
    return pl.pallas_call(
        functools.partial(_mm_heads_body, hb=hb, slot=slot),
        grid=(n // tn, m // tm),
        in_specs=in_specs,
        out_specs=[pl.BlockSpec((tm, tn), lambda j, i: (i, j)), o5_spec],
        out_shape=[jax.ShapeDtypeStruct((m, n), BF16),
                   jax.ShapeDtypeStruct((batch, n_slots, seq, heads, LANES), F32)],
        input_output_aliases=aliases,
        compiler_params=_params(("parallel", "parallel"),
                                [((tm, k), BF16), ((k, tn), BF16), ((tm, tn), BF16), o5_block]),
        name=name,
    )(*args)


def _swiglu(a, wa_ref, wb_ref):
    ga = jnp.dot(a, wa_ref[...], preferred_element_type=F32)
    gb = jnp.dot(a, wb_ref[...], preferred_element_type=F32)
    return jax.nn.silu(ga) * gb


def _swiglu_body(a_ref, wa_ref, wb_ref, *refs, has_sample):
    n_round = (len(refs) - 1 - 2 * has_sample) // 2
    src = refs[has_sample:has_sample + n_round]
    o_ref = refs[has_sample + n_round]
    dst = refs[2 * has_sample + n_round + 1:]
    o_ref[...] = _swiglu(a_ref[...], wa_ref, wb_ref).astype(o_ref.dtype)
    _round_blocks(src, dst)
    if has_sample:
        as_ref, os_ref = refs[0], refs[has_sample + n_round + 1]

        @pl.when(pl.program_id(0) == 0)
        def _():
            os_ref[...] = _swiglu(as_ref[...], wa_ref, wb_ref).astype(os_ref.dtype)


def _matmul_swiglu(a, w13, *, tm, tn, rounding=(), sample=None, name):
    m, k = a.shape
    hidden = w13.shape[1] // 2
    assert m % tm == 0 and hidden % tn == 0
    nj = hidden // tn
    r_in, r_out, r_shape, r_blocks = _rounding_specs(rounding, (m // tm) * nj, lambda i, j: i * nj + j)
    s_in, s_out, s_shape, s_args = [], [], [], []
    if sample is not None:
        ms = sample.shape[0]
        s_in = [pl.BlockSpec((ms, k), lambda i, j: (0, 0))]
        s_out = [pl.BlockSpec((ms, tn), lambda i, j: (0, jnp.where(i == 0, j, nj - 1)))]
        s_shape = [jax.ShapeDtypeStruct((ms, hidden), BF16)]
        s_args = [sample]
        r_blocks = r_blocks + [((ms, k), BF16), ((ms, tn), BF16)]
    res = pl.pallas_call(
        functools.partial(_swiglu_body, has_sample=sample is not None),
        grid=(m // tm, nj),
        in_specs=[pl.BlockSpec((tm, k), lambda i, j: (i, 0)),
                  pl.BlockSpec((k, tn), lambda i, j: (0, j)),
                  pl.BlockSpec((k, tn), lambda i, j: (0, j + nj))] + s_in + r_in,
        out_specs=[pl.BlockSpec((tm, tn), lambda i, j: (i, j))] + s_out + r_out,
        out_shape=[jax.ShapeDtypeStruct((m, hidden), BF16)] + s_shape + r_shape,
        compiler_params=_params(("arbitrary", "arbitrary"),
                                [((tm, k), BF16), ((k, tn), BF16), ((k, tn), BF16), ((tm, tn), BF16)] + r_blocks),
        name=name,
    )(a, w13, w13, *s_args, *[w for w, _ in rounding])
    return res[0] if len(res) == 1 else tuple(res)


def _sb_prompt_body(bias_ref, q_ref, k_ref, v_ref, *refs, tq, hp):
    n_round = (len(refs) - 1) // 2
    o_ref = refs[n_round]
    _round_blocks(refs[:n_round], refs[n_round + 1:])
    qi = pl.program_id(2)
    d = LANES
    row = lax.broadcasted_iota(jnp.int32, (tq, tq), 0)
    col = lax.broadcasted_iota(jnp.int32, (tq, tq), 1)
    causal = col < row
    gw = min(tq, MXU_WIDTH)
    later = (lax.broadcasted_iota(jnp.int32, (gw, gw), 0)
             > lax.broadcasted_iota(jnp.int32, (gw, gw), 1)).astype(BF16)

    def block(kb, h, acc, carry, mask):
        hs = slice(h * d, (h + 1) * d)
        ks = k_ref[pl.ds(kb * tq, tq), hs]
        vs = v_ref[pl.ds(kb * tq, tq), hs]
        z = lax.dot_general(q_ref[:, hs], ks, NT_DIMS, preferred_element_type=F32)
        z = z + bias_ref[h]
        log_keep = _neg_softplus(z)
        if mask:
            log_keep = jnp.where(causal, log_keep, 0.0)
        after = _sum_over_later(log_keep, later, 2)
        w = jnp.exp(log_keep + z + after + carry)
        if mask:
            w = jnp.where(causal, w, 0.0)
        acc = acc + jnp.dot(w.astype(BF16), vs, preferred_element_type=F32)
        carry = carry + after[:, :1] + log_keep[:, :1]
        return acc, carry

    state = []
    for h in range(hp):
        state.extend(block(qi, h, jnp.zeros((tq, d), F32), jnp.zeros((tq, 1), F32), True))

    def body(it, c):
        out = []
        for h in range(hp):
            out.extend(block(qi - 1 - it, h, c[2 * h], c[2 * h + 1], False))
        return tuple(out)

    state = lax.fori_loop(0, qi, body, tuple(state))
    for h in range(hp):
        o_ref[:, h * d:(h + 1) * d] = state[2 * h].astype(o_ref.dtype)


def _sb_prompt(q, k, v, bias, *, batch, seq, heads, tq, hp, rounding=(), name):
    d = LANES
    nq = seq // tq
    ng = heads // hp
    bias_rows = jnp.broadcast_to(bias.astype(F32)[:, None, None], (heads, 1, tq))
    q_spec = pl.BlockSpec((tq, hp * d), lambda b, h, i: (b * nq + i, h))
    kv_spec = pl.BlockSpec((seq, hp * d), lambda b, h, i: (b, h))
    r_in, r_out, r_shape, r_blocks = _rounding_specs(rounding, batch * ng * nq,
                                                     lambda b, h, i: (b * ng + h) * nq + i)
    res = pl.pallas_call(
        functools.partial(_sb_prompt_body, tq=tq, hp=hp),
        grid=(batch, ng, nq),
        in_specs=[pl.BlockSpec((hp, 1, tq), lambda b, h, i: (h, 0, 0)), q_spec, kv_spec, kv_spec] + r_in,
        out_specs=[q_spec] + r_out,
        out_shape=[jax.ShapeDtypeStruct((batch * seq, heads * d), BF16)] + r_shape,
        compiler_params=_params(("arbitrary", "arbitrary", "arbitrary"),
                                [((tq, hp * d), BF16)] * 2 + [((seq, hp * d), BF16)] * 2 + r_blocks),
        name=name,
    )(bias_rows, q, k, v, *[w for w, _ in rounding])
    return res[0] if not rounding else tuple(res)


def _sb_sample_body(pt_ref, q_ref, bias_ref, *refs, heads, pps, scale):
    del pt_ref
    k_refs, v_refs = refs[:pps], refs[pps:2 * pps]
    o_ref, acc_ref, carry_ref = refs[2 * pps:]
    pg = pl.program_id(1)
    page = k_refs[0].shape[0]
    d = LANES

    @pl.when(pg == 0)
    def _():
        acc_ref[...] = jnp.zeros_like(acc_ref)
        carry_ref[...] = jnp.zeros_like(carry_ref)

    row = lax.broadcasted_iota(jnp.int32, (page, page), 0)
    col = lax.broadcasted_iota(jnp.int32, (page, page), 1)
    later = (row > col).astype(BF16)
    hrow = lax.broadcasted_iota(jnp.int32, (heads, page), 0)
    hrow_d = lax.broadcasted_iota(jnp.int32, (heads, d), 0)
    q = q_ref[...]
    bias = bias_ref[...]
    acc = acc_ref[...]
    carry = carry_ref[...]
    for k_ref, v_ref in zip(k_refs, v_refs):
        kt = jnp.swapaxes(k_ref[...], 0, 1).astype(BF16)
        vt = jnp.swapaxes(v_ref[...], 0, 1).astype(BF16)
        z = jnp.zeros((heads, page), F32)
        for h in range(heads):
            zh = lax.dot_general(q, kt[h], NT_DIMS, preferred_element_type=F32)
            z = jnp.where(hrow == h, zh, z)
        z = z * scale + bias
        log_keep = _neg_softplus(z)
        after = _sum_over_later(log_keep, later, 3)
        w = jnp.exp(log_keep + z + after + carry).astype(BF16)
        carry = carry + after[:, :1] + log_keep[:, :1]
        for h in range(heads):
            oh = jnp.dot(w, vt[h], preferred_element_type=F32)
            acc = acc + jnp.where(hrow_d == h, oh, 0.0)
    acc_ref[...] = acc
    carry_ref[...] = carry

    @pl.when(pg == pl.num_programs(1) - 1)
    def _():
        o_ref[...] = acc.astype(o_ref.dtype)


def _sb_sample(q, bias, cache_k, cache_v, page_table, layer, *, pps, name):
    bsz, heads, d = q.shape
    page = cache_k.shape[2]
    n_pages = page_table.shape[1]
    assert n_pages % pps == 0
    bias_rows = jnp.broadcast_to(bias.astype(F32)[:, None], (heads, page))

    def cache_spec(r):
        return pl.BlockSpec((None, None, page, heads, d),
                            lambda b, p, pt: (pt[b, n_pages - 1 - (p * pps + r)], layer, 0, 0, 0))

    grid_spec = pltpu.PrefetchScalarGridSpec(
        num_scalar_prefetch=1,
        grid=(bsz, n_pages // pps),
        in_specs=[pl.BlockSpec((None, heads, d), lambda b, p, pt: (b, 0, 0)),
                  pl.BlockSpec((heads, page), lambda b, p, pt: (0, 0))]
                 + [cache_spec(r) for r in range(pps)] * 2,
        out_specs=pl.BlockSpec((None, heads, d), lambda b, p, pt: (b, 0, 0)),
        scratch_shapes=[pltpu.VMEM((heads, d), F32), pltpu.VMEM((heads, 1), F32)],
    )
    out = pl.pallas_call(
        functools.partial(_sb_sample_body, heads=heads, pps=pps, scale=d ** -0.5),
        grid_spec=grid_spec,
        out_shape=jax.ShapeDtypeStruct((bsz, heads, d), BF16),
        compiler_params=_params(("parallel", "arbitrary"), [((page, heads, d), F32)] * (2 * pps)),
        name=name,
    )(page_table, q.astype(BF16), bias_rows, *([cache_k] * pps), *([cache_v] * pps))
    return out.reshape(bsz, heads * d)


def _gmlp_prompt_body(u_ref, vb_ref, gn_ref, ws_ref, bst_ref, o_ref, *, groups):
    chunk = u_ref.shape[0]
    vn = _rms_rows(jax.nn.gelu(vb_ref[...]), gn_ref[...])
    row = lax.broadcasted_iota(jnp.int32, (chunk, chunk), 0)
    col = lax.broadcasted_iota(jnp.int32, (chunk, chunk), 1)
    for g in range(groups):
        sl = slice(g * LANES, (g + 1) * LANES)
        w = jnp.where(col <= row, ws_ref[g], 0.0).astype(BF16)
        s = jnp.dot(w, vn[:, sl].astype(BF16), preferred_element_type=F32) + bst_ref[:, g:g + 1]
        o_ref[:, sl] = (jax.nn.gelu(u_ref[:, sl]) * s).astype(o_ref.dtype)


def _gmlp_prompt(p, gnorm, ws, bs, *, u_col, name):
    groups, chunk, _ = ws.shape
    width = groups * LANES
    m = p.shape[0]
    return pl.pallas_call(
        functools.partial(_gmlp_prompt_body, groups=groups),
        grid=(m // chunk,),
        in_specs=[pl.BlockSpec((chunk, width), lambda i: (i, u_col)),
                  pl.BlockSpec((chunk, width), lambda i: (i, u_col + 1)),
                  pl.BlockSpec((1, width), lambda i: (0, 0)),
                  pl.BlockSpec((groups, chunk, chunk), lambda i: (0, 0, 0)),
                  pl.BlockSpec((chunk, groups), lambda i: (0, 0))],
        out_specs=pl.BlockSpec((chunk, width), lambda i: (i, 0)),
        out_shape=jax.ShapeDtypeStruct((m, width), BF16),
        compiler_params=_params(("parallel",),
                                [((chunk, width), F32)] * 2 + [((groups, chunk, chunk), F32), ((chunk, width), BF16)]),
        name=name,
    )(p, p, gnorm.reshape(1, width), ws, bs.T)


def _gmlp_sample_body(u_ref, vb_ref, gn_ref, w0_ref, b0_ref, o_ref, vn_ref):
    vn = _rms_rows(jax.nn.gelu(vb_ref[...]), gn_ref[...])
    vn_ref[...] = vn
    o_ref[...] = (jax.nn.gelu(u_ref[...]) * (w0_ref[...] * vn + b0_ref[...])).astype(o_ref.dtype)


def _gmlp_sample(p, gnorm, ws, bs, *, u_col, name):
    groups = ws.shape[0]
    width = groups * LANES
    m = p.shape[0]
    w0 = jnp.repeat(ws[:, 0, 0], LANES).reshape(1, width)
    b0 = jnp.repeat(bs[:, 0], LANES).reshape(1, width)
    row = lambda c: pl.BlockSpec((m, width), lambda i: (0, c))
    vec = pl.BlockSpec((1, width), lambda i: (0, 0))
    return pl.pallas_call(
        _gmlp_sample_body,
        grid=(1,),
        in_specs=[row(u_col), row(u_col + 1), vec, vec, vec],
        out_specs=[row(0), row(0)],
        out_shape=[jax.ShapeDtypeStruct((m, width), BF16), jax.ShapeDtypeStruct((m, width), F32)],
        compiler_params=_params(("arbitrary",), [((m, width), F32)] * 4),
        name=name,
    )(p, p, gnorm.reshape(1, width), w0, b0)


HGRN_LEVELS = 7


def _hgrn_gates(fpre, lb):
    t = jnp.log(1.0 + jnp.exp(-jnp.abs(fpre)))
    log_sig = jnp.minimum(fpre, 0.0) - t
    log_sig_neg = jnp.minimum(-fpre, 0.0) - t
    a = jnp.log(lb)
    c = jnp.log1p(-lb) + log_sig
    logf = jnp.maximum(a, c) + jnp.log(1.0 + jnp.exp(-jnp.abs(a - c)))
    return logf, (1.0 - lb) * jnp.exp(log_sig_neg)


def _hgrn_finish(o, g, gn):
    return _rms_rows(o, gn) * jax.nn.silu(g)


def _hgrn_tables():
    n = LANES
    t = np.arange(n)[:, None]
    r = np.arange(n)[None, :]
    blocks = []
    for level in range(HGRN_LEVELS):
        c = 1 << level
        mid = (t // (2 * c)) * (2 * c) + c
        second = (t % (2 * c)) >= c
        blocks.append(np.where(second, (r >= mid) & (r <= t), (r > t) & (r < mid)))
    blocks += [r <= t, r > t, np.ones((SUBLANES, n), bool)]
    ranges = np.concatenate(blocks, axis=0).astype(np.float32)
    diff = t ^ r
    level_of = np.where(t == r, -1, np.where(r < t, np.floor(np.log2(np.maximum(diff, 1))), HGRN_LEVELS))
    return (jnp.asarray(np.concatenate([ranges, ranges], axis=1), BF16),
            jnp.asarray(level_of.astype(np.int32)))


def _hgrn_prompt_body(q_ref, f_ref, i_ref, g_ref, lb_ref, gn_ref, rng_ref, lvl_ref, *refs, tb):
    n_round = (len(refs) - 3) // 2
    o_ref, s_ref = refs[n_round:n_round + 2]
    st_ref = refs[-1]
    _round_blocks(refs[:n_round], refs[n_round + 2:-1])
    step = pl.program_id(2)
    n = LANES

    @pl.when(step == 0)
    def _():
        st_ref[...] = jnp.zeros_like(st_ref)

    lb = lb_ref[...]
    gn = gn_ref[...]
    level_of = lvl_ref[...]
    st = st_ref[...]
    tiles = [slice(i * n, (i + 1) * n) for i in range(tb // n)]
    logf_all, k_all = _hgrn_gates(f_ref[...], lb)
    rhs = jnp.concatenate([jnp.concatenate([piece[rs] for rs in tiles], axis=1)
                           for piece in _bf16_pieces(logf_all, 2)], axis=0)
    decay_all = jnp.exp(jnp.dot(rng_ref[...], rhs, preferred_element_type=F32))
    for rs in tiles:
        q = q_ref[rs, :]
        v = i_ref[rs, :]
        k = k_all[rs]
        decay = decay_all[:, rs]
        qb, kb, vb = q.astype(BF16), k.astype(BF16), v.astype(BF16)
        scores = jnp.zeros((n, n), F32)
        for level in range(HGRN_LEVELS):
            dl = decay[level * n:(level + 1) * n].astype(BF16)
            pair = lax.dot_general(qb * dl, kb * dl, NT_DIMS, preferred_element_type=F32)
            scores = jnp.where(level_of == level, pair, scores)
        same = lax.dot_general(qb, kb, NT_DIMS, preferred_element_type=F32)
        scores = jnp.where(level_of == -1, same, scores)
        from_start = decay[HGRN_LEVELS * n:(HGRN_LEVELS + 1) * n].astype(BF16)
        to_end = decay[(HGRN_LEVELS + 1) * n:(HGRN_LEVELS + 2) * n].astype(BF16)
        total = decay[(HGRN_LEVELS + 2) * n:(HGRN_LEVELS + 2) * n + 1]
        o = (jnp.dot(scores.astype(BF16), vb, preferred_element_type=F32)
             + lax.dot_general(qb * from_start, st.astype(BF16), NT_DIMS, preferred_element_type=F32))
        st = st * total + lax.dot_general(vb, kb * to_end, TN_DIMS, preferred_element_type=F32)
        o_ref[rs, :] = _hgrn_finish(o, g_ref[rs, :], gn).astype(o_ref.dtype)
    st_ref[...] = st

    @pl.when(step == pl.num_programs(2) - 1)
    def _():
        s_ref[...] = st.T


def _hgrn_prompt(p, lb, gnorm, *, batch, seq, heads, tb, rounding=(), name):
    d = LANES
    nt = seq // tb
    ranges, level_of = _hgrn_tables()
    col = lambda c: pl.BlockSpec((tb, d), lambda b, h, t: (b * nt + t, c * heads + h))
    r_in, r_out, r_shape, r_blocks = _rounding_specs(rounding, batch * heads * nt,
                                                     lambda b, h, t: (b * heads + h) * nt + t)
    res = pl.pallas_call(
        functools.partial(_hgrn_prompt_body, tb=tb),
        grid=(batch, heads, nt),
        in_specs=[col(0), col(1), col(2), col(3),
                  pl.BlockSpec((None, 1, d), lambda b, h, t: (h, 0, 0)),
                  pl.BlockSpec((1, d), lambda b, h, t: (0, 0)),
                  pl.BlockSpec(ranges.shape, lambda b, h, t: (0, 0)),
                  pl.BlockSpec(level_of.shape, lambda b, h, t: (0, 0))] + r_in,
        out_specs=[pl.BlockSpec((tb, d), lambda b, h, t: (b * nt + t, h)),
                   pl.BlockSpec((None, None, d, d), lambda b, h, t: (b, h, 0, 0))] + r_out,
        out_shape=[jax.ShapeDtypeStruct((batch * seq, heads * d), BF16),
                   jax.ShapeDtypeStruct((batch, heads, d, d), F32)] + r_shape,
        scratch_shapes=[pltpu.VMEM((d, d), F32)],
        compiler_params=_params(("arbitrary", "arbitrary", "arbitrary"),
                                [((tb, d), F32)] * 5 + [((d, d), F32), (ranges.shape, BF16)] + r_blocks),
        name=name,
    )(p, p, p, p, lb.reshape(heads, 1, d), gnorm.reshape(1, d), ranges, level_of, *[w for w, _ in rounding])
    return tuple(res)


def _hgrn_sample_body(q_ref, f_ref, i_ref, g_ref, lb_ref, gn_ref, s_ref, o_ref, so_ref, oacc_ref, *, heads):
    d = LANES
    q = q_ref[...]
    v = i_ref[...]
    logf, k = _hgrn_gates(f_ref[...], lb_ref[...])
    f = jnp.exp(logf)
    eye = lax.broadcasted_iota(jnp.int32, (d, d), 0) == lax.broadcasted_iota(jnp.int32, (d, d), 1)

    def to_col(r):
        return jnp.sum(jnp.where(eye, r, 0.0), axis=1, keepdims=True)

    for h in range(heads):
        hs = slice(h, h + 1)
        s_new = to_col(f[hs]) * s_ref[h] + to_col(k[hs]) * v[hs]
        so_ref[h] = s_new
        oacc_ref[hs, :] = jnp.sum(to_col(q[hs]) * s_new, axis=0, keepdims=True)
    o_ref[...] = _hgrn_finish(oacc_ref[...], g_ref[...], gn_ref[...]).astype(o_ref.dtype)


def _hgrn_sample(p, lb, gnorm, state, layer, *, heads, name):
    bsz = p.shape[0]
    d = LANES
    p4 = p.reshape(bsz, 4, heads, d)
    part = lambda c: pl.BlockSpec((None, None, heads, d), lambda b: (b, c, 0, 0))
    o, s_new = pl.pallas_call(
        functools.partial(_hgrn_sample_body, heads=heads),
        grid=(bsz,),
        in_specs=[part(0), part(1), part(2), part(3),
                  pl.BlockSpec((heads, d), lambda b: (0, 0)),
                  pl.BlockSpec((1, d), lambda b: (0, 0)),
                  pl.BlockSpec((None, None, heads, d, d), lambda b: (layer, b, 0, 0, 0))],
        out_specs=[pl.BlockSpec((None, heads, d), lambda b: (b, 0, 0)),
                   pl.BlockSpec((None, heads, d, d), lambda b: (b, 0, 0, 0))],
        out_shape=[jax.ShapeDtypeStruct((bsz, heads, d), BF16),
                   jax.ShapeDtypeStruct((bsz, heads, d, d), F32)],
        scratch_shapes=[pltpu.VMEM((heads, d), F32)],
        compiler_params=_params(("parallel",), [((heads, d, d), F32)] * 2),
        name=name,
    )(p4, p4, p4, p4, lb.reshape(heads, d), gnorm.reshape(1, d), state)
    return o.reshape(bsz, heads * d), s_new


def _attend(q, k_ref, v_ref, heads, scale):
    outs = []
    for h in range(heads):
        sl = slice(h * LANES, (h + 1) * LANES)
        s = lax.dot_general(q[:, sl].astype(BF16), k_ref[:, sl].astype(BF16), NT_DIMS,
                            preferred_element_type=F32) * scale
        e = jnp.exp(s - jnp.max(s, axis=-1, keepdims=True))
        prob = e * (1.0 / jnp.sum(e, axis=-1, keepdims=True))
        outs.append(jnp.dot(prob.astype(BF16), v_ref[:, sl].astype(BF16), preferred_element_type=F32))
    return jnp.concatenate(outs, axis=1)


def _xattn_prompt_body(x_ref, gx_ref, gf_ref, wq_ref, wo_ref, k_ref, v_ref, y_ref, hf_ref, *, heads, scale):
    x = x_ref[...]
    h = _rms_rows(x, gx_ref[...]).astype(BF16)
    q = jnp.dot(h, wq_ref[...], preferred_element_type=F32)
    att = _attend(q, k_ref, v_ref, heads, scale).astype(BF16)
    y = x + jnp.dot(att, wo_ref[...], preferred_element_type=F32)
    y_ref[...] = y
    hf_ref[...] = _rms_rows(y, gf_ref[...]).astype(hf_ref.dtype)


def _xattn_prompt(x, gx, gf, wq, wo, layer, mem_k, mem_v, *, batch, seq, tq, name):
    d = x.shape[1]
    xw = wq.shape[2]
    heads = xw // LANES
    mlen = mem_k.shape[1]
    nt = seq // tq
    rows = pl.BlockSpec((tq, d), lambda b, t: (b * nt + t, 0))
    vec = pl.BlockSpec((1, d), lambda b, t: (0, 0))
    mem = pl.BlockSpec((None, mlen, xw), lambda b, t: (b, 0, 0))
    return pl.pallas_call(
        functools.partial(_xattn_prompt_body, heads=heads, scale=LANES ** -0.5),
        grid=(batch, nt),
        in_specs=[rows, vec, vec,
                  pl.BlockSpec((None, d, xw), lambda b, t: (layer, 0, 0)),
                  pl.BlockSpec((None, xw, d), lambda b, t: (layer, 0, 0)),
                  mem, mem],
        out_specs=[rows, rows],
        out_shape=[jax.ShapeDtypeStruct(x.shape, F32), jax.ShapeDtypeStruct(x.shape, BF16)],
        compiler_params=_params(("parallel", "parallel"),
                                [((tq, d), F32)] * 2 + [((tq, d), BF16), ((d, xw), BF16), ((xw, d), BF16),
                                                        ((mlen, xw), F32), ((mlen, xw), F32)]),
        name=name,
    )(x, gx.reshape(1, d), gf.reshape(1, d), wq, wo, mem_k, mem_v)


def _xattn_sample_body(x_ref, gx_ref, gf_ref, wq_ref, wo_ref, k_ref, v_ref, y_ref, hf_ref, q_sc, att_sc,
                       *, heads, scale):
    b = pl.program_id(0)

    @pl.when(b == 0)
    def _():
        h = _rms_rows(x_ref[...], gx_ref[...]).astype(BF16)
        q_sc[...] = jnp.dot(h, wq_ref[...], preferred_element_type=F32)
        att_sc[...] = jnp.zeros_like(att_sc)

    att = _attend(q_sc[...], k_ref, v_ref, heads, scale)
    mine = lax.broadcasted_iota(jnp.int32, att.shape, 0) == b
    att_sc[...] += jnp.where(mine, att, 0.0)

    @pl.when(b == pl.num_programs(0) - 1)
    def _():
        y = x_ref[...] + jnp.dot(att_sc[...].astype(BF16), wo_ref[...], preferred_element_type=F32)
        y_ref[...] = y
        hf_ref[...] = _rms_rows(y, gf_ref[...]).astype(hf_ref.dtype)


def _xattn_sample(x, gx, gf, wq, wo, mem_k, mem_v, layer, name):
    bsz, d = x.shape
    xw = wq.shape[2]
    heads = xw // LANES
    mlen = mem_k.shape[2]
    mk = mem_k.reshape(mem_k.shape[0], bsz, mlen, xw)
    mv = mem_v.reshape(mem_v.shape[0], bsz, mlen, xw)
    rows = pl.BlockSpec((bsz, d), lambda b: (0, 0))
    vec = pl.BlockSpec((1, d), lambda b: (0, 0))
    mem = pl.BlockSpec((None, None, mlen, xw), lambda b: (layer, b, 0, 0))
    return pl.pallas_call(
        functools.partial(_xattn_sample_body, heads=heads, scale=LANES ** -0.5),
        grid=(bsz,),
        in_specs=[rows, vec, vec,
                  pl.BlockSpec((None, d, xw), lambda b: (layer, 0, 0)),
                  pl.BlockSpec((None, xw, d), lambda b: (layer, 0, 0)),
                  mem, mem],
        out_specs=[rows, rows],
        out_shape=[jax.ShapeDtypeStruct(x.shape, F32), jax.ShapeDtypeStruct(x.shape, BF16)],
        scratch_shapes=[pltpu.VMEM((bsz, xw), F32), pltpu.VMEM((bsz, xw), F32)],
        compiler_params=_params(("arbitrary",),
                                [((d, xw), BF16), ((xw, d), BF16), ((mlen, xw), F32), ((mlen, xw), F32)]),
        name=name,
    )(x, gx.reshape(1, d), gf.reshape(1, d), wq, wo, mk, mv)


def kernel(x_prompt, x_sample, mem_prompt, cache_sb_k, cache_sb_v, cache_mem_k, cache_mem_v, state_hgrn, page_table, norm_mix, norm_mem, norm_xattn, norm_ffn, norm_final, w_in_ab, w_out_ab, sb_bias, gmlp_norm, gmlp_ws, gmlp_bs, w_in_c, w_out_c, hgrn_lb, hgrn_gnorm, xattn_wq, xattn_wkv, xattn_wo, ffn_w13, ffn_w2):
    bp, tp, d = x_prompt.shape
    bd, ts, _ = x_sample.shape
    assert ts == 1, "the sample group carries one new token per sequence"
    depth = norm_mix.shape[0]
    n_even = w_in_ab.shape[0]
    a_heads = sb_bias.shape[1]
    a_width = a_heads * LANES
    groups = gmlp_ws.shape[1]
    b_width = groups * LANES
    c_heads = hgrn_lb.shape[1] // LANES
    mem_len = mem_prompt.shape[1]
    xw = xattn_wq.shape[2]
    assert a_width == b_width and w_in_ab.shape[2] == 3 * a_width + 2 * b_width

    xp = x_prompt.reshape(bp * tp, d)
    xs = x_sample.reshape(bd, d)
    mem = mem_prompt.reshape(bp * mem_len, d)

    lb_all = jnp.cumsum(jax.nn.softmax(hgrn_lb.astype(F32), axis=0), axis=0)
    lb_all = lb_all - lb_all[:1]

    tm_p, tm_s = 1024, bd
    ffn_tn = 256

    w_in = w_in_ab[0].astype(BF16)
    wq_all = xattn_wq.astype(BF16)
    wo_all = xattn_wo.astype(BF16)
    wkv_all = xattn_wkv.astype(BF16)

    sbk_p, sbv_p = None, None
    sbk_s, sbv_s, gv_s, hs_p, hs_s, mk_p, mv_p = [], [], [], [], [], [], []
    for l in range(depth):
        j = l // 2
        hp = _rmsnorm(xp, norm_mix[l], BF16, "norm_mix_p")
        hs = _rmsnorm(xs, norm_mix[l], BF16, "norm_mix_s")
        if l % 2 == 0:
            qp = _matmul([hp], w_in, w_cols=(0, a_width), out_dtype=BF16, tm=tm_p, tn=1024,
                         out_scale=LANES ** -0.5, name="in_q_p")
            kp2, sbk_p = _matmul_heads(hp, w_in, layer=None, w_cols=(a_width, a_width), batch=bp, seq=tp, slot=j,
                                       n_slots=n_even, stacked=sbk_p, tm=tm_p, name="in_k_p")
            vp2, sbv_p = _matmul_heads(hp, w_in, layer=None, w_cols=(2 * a_width, a_width), batch=bp, seq=tp,
                                       slot=j, n_slots=n_even, stacked=sbv_p, tm=tm_p, name="in_v_p")
            uv = _matmul([hp], w_in, w_cols=(3 * a_width, 2 * b_width), out_dtype=F32, tm=tm_p, tn=1024,
                         name="in_uv_p")
            oa, w13, w_out = _sb_prompt(qp, kp2, vp2, sb_bias[j], batch=bp, seq=tp, heads=a_heads, tq=512, hp=4,
                                        rounding=[(ffn_w13, l), (w_out_ab, j)], name="sb_p")
            ob = _gmlp_prompt(uv, gmlp_norm[j], gmlp_ws[j], gmlp_bs[j], u_col=0, name="gmlp_p")
            ps = _matmul([hs], w_in, out_dtype=F32, tm=tm_s, tn=1024, name="in_ab_s")
            oa_s = _sb_sample(ps[:, :a_width].reshape(bd, a_heads, LANES), sb_bias[j], cache_sb_k, cache_sb_v,
                              page_table, j, pps=8, name="sb_s")
            ob_s, vn = _gmlp_sample(ps, gmlp_norm[j], gmlp_ws[j], gmlp_bs[j], u_col=3 * a_width // b_width,
                                    name="gmlp_s")
            xp, xs = _matmul([oa, ob], w_out, out_dtype=F32, tm=tm_p, tn=1024, resid=xp,
                             sample=([oa_s, ob_s], xs), name="out_ab")
            sbk_s.append(ps[:, a_width:2 * a_width].reshape(bd, ts, a_heads, LANES))
            sbv_s.append(ps[:, 2 * a_width:3 * a_width].reshape(bd, ts, a_heads, LANES))
            gv_s.append(vn.reshape(bd, ts, groups, LANES))
        else:
            pp, ps = _matmul([hp], w_in, out_dtype=F32, tm=tm_p, tn=1024, sample=([hs], None), name="in_c")
            oc, s_p, w13, w_out = _hgrn_prompt(pp, lb_all[j], hgrn_gnorm[j], batch=bp, seq=tp, heads=c_heads,
                                               tb=tp, rounding=[(ffn_w13, l), (w_out_c, j)], name="hgrn_p")
            oc_s, s_s = _hgrn_sample(ps, lb_all[j], hgrn_gnorm[j], state_hgrn, j, heads=c_heads, name="hgrn_s")
            xp, xs = _matmul([oc], w_out, out_dtype=F32, tm=tm_p, tn=1024, resid=xp, sample=([oc_s], xs),
                             name="out_c")
            hs_p.append(s_p)
            hs_s.append(s_s)
        hm = _rmsnorm(mem, norm_mem[l], BF16, "norm_mem")
        kv = _matmul([hm], wkv_all, layer=l, out_dtype=F32, tm=bp * mem_len, tn=512, name="mem_kv")
        kp = kv[:, :xw].reshape(bp, mem_len, xw)
        vp = kv[:, xw:].reshape(bp, mem_len, xw)
        mk_p.append(kp.reshape(bp, mem_len, xw // LANES, LANES))
        mv_p.append(vp.reshape(bp, mem_len, xw // LANES, LANES))
        xp, hfp = _xattn_prompt(xp, norm_xattn[l], norm_ffn[l], wq_all, wo_all, l, kp, vp, batch=bp, seq=tp, tq=512,
                                name="xattn_p")
        xs, hfs = _xattn_sample(xs, norm_xattn[l], norm_ffn[l], wq_all, wo_all, cache_mem_k, cache_mem_v, l,
                                "xattn_s")
        nxt = [] if l + 1 == depth else [(w_in_c, (l + 1) // 2)] if l % 2 == 0 else [(w_in_ab, (l + 1) // 2)]
        mid, mid_s, w2, *w_next = _matmul_swiglu(hfp, w13, tm=2 * tm_p, tn=ffn_tn, sample=hfs,
                                                 rounding=[(ffn_w2, l)] + nxt, name="ffn13")
        if w_next:
            w_in = w_next[0]
        xp, xs = _matmul([mid], w2, out_dtype=F32, tm=512, tn=1024, resid=xp, w_single=True,
                         sample=([mid_s], xs), name="ffn2")
    y_prompt = _rmsnorm(xp, norm_final, F32, "norm_final_p").reshape(bp, tp, d)
    y_sample = _rmsnorm(xs, norm_final, F32, "norm_final_s").reshape(bd, ts, d)
    return (y_prompt, y_sample, sbk_p, sbv_p,
            jnp.stack(sbk_s, axis=1), jnp.stack(sbv_s, axis=1),
            jnp.stack(gv_s, axis=1),
            jnp.stack(hs_p, axis=0), jnp.stack(hs_s, axis=0),
            jnp.stack(mk_p, axis=0), jnp.stack(mv_p, axis=0))
```
